```python
import jax, jax.numpy as jnp
from jax import lax
import numpy as np

D_MODEL = 1024
BATCH = 16
SEQ = 256
DEPTH = 1
DEC_BATCH = 4
DEC_SEQ = 4096
PAST_LEN = 256

GRID_W = 64
H_A = 8
DK_A = 128
DV_A = 128
D_QK = H_A * DK_A
D_VA = H_A * DV_A
CHUNK = 64
CONV_K = 4
CONV_PAD_L = 2
D_B = 1024
NB_B = 8
BW_B = D_B // NB_B
LRU_C = 8.0
EPS = 1e-6
OFF_QKV = 2 * D_QK + D_VA
OFF_ZA = OFF_QKV + D_VA
OFF_XB = OFF_ZA + D_B
OFF_ZB = OFF_XB + D_B
IN_COLS = OFF_ZB + 4 * H_A

kernel_name = 'hybrid_gdn_rglru_prefix_diffusion_step'


def rmsnorm(x, w):
    xf = x.astype(jnp.float32)
    y = xf * lax.rsqrt(jnp.mean(xf * xf, axis=-1, keepdims=True) + EPS)
    return (y * w.astype(jnp.float32)).astype(x.dtype)


def l2norm(x):
    xf = x.astype(jnp.float32)
    return xf * lax.rsqrt(jnp.sum(xf * xf, axis=-1, keepdims=True) + EPS)


def dwconv_centred(x, w, b):
    n = x.shape[1]
    xp = jnp.pad(x, ((0, 0), (CONV_PAD_L, CONV_K - 1 - CONV_PAD_L), (0, 0)))
    y = xp[:, 0:n] * w[0]
    for i in range(1, CONV_K):
        y = y + xp[:, i:i + n] * w[i]
    return y + b


def delta_rule_chunked(q, k, v, g, beta, s0):
    b, h, n, dk = q.shape
    dv = v.shape[-1]
    nc = n // CHUNK
    f32 = jnp.float32
    q = (q.astype(f32) * (dk ** -0.5)).reshape(b, h, nc, CHUNK, dk)
    k = k.astype(f32).reshape(b, h, nc, CHUNK, dk)
    v = v.astype(f32).reshape(b, h, nc, CHUNK, dv)
    g = jnp.cumsum(g.astype(f32).reshape(b, h, nc, CHUNK), axis=-1)
    beta = beta.astype(f32).reshape(b, h, nc, CHUNK, 1)
    idx = jnp.arange(CHUNK)
    incl = idx[:, None] >= idx[None, :]
    strict = idx[:, None] > idx[None, :]
    decay = jnp.exp(jnp.where(incl, g[..., :, None] - g[..., None, :], -jnp.inf))
    kb = k * beta
    a_mat = jnp.where(strict, jnp.einsum('bhnik,bhnjk->bhnij', kb, k) * decay, 0.0)
    lower = a_mat + jnp.eye(CHUNK, dtype=f32)
    rhs = jnp.concatenate([v * beta, kb * jnp.exp(g)[..., None]], axis=-1)
    sol = lax.linalg.triangular_solve(lower, rhs, left_side=True, lower=True, unit_diagonal=True)
    u, w = sol[..., :dv], sol[..., dv:]
    qk = jnp.einsum('bhnik,bhnjk->bhnij', q, k) * decay
    qg = q * jnp.exp(g)[..., None]
    g_last = g[..., -1]
    kd = k * jnp.exp(g_last[..., None] - g)[..., None]
    decay_last = jnp.exp(g_last)

    def step(s, xs):
        w_c, u_c, qg_c, qk_c, kd_c, dl_c = xs
        v_new = u_c - jnp.einsum('bhck,bhkv->bhcv', w_c, s)
        o_c = jnp.einsum('bhck,bhkv->bhcv', qg_c, s) + jnp.einsum('bhcj,bhjv->bhcv', qk_c, v_new)
        s = s * dl_c[..., None, None] + jnp.einsum('bhck,bhcv->bhkv', kd_c, v_new)
        return s, o_c

    xs = tuple(jnp.moveaxis(t, 2, 0) for t in (w, u, qg, qk, kd, decay_last))
    s_fin, o = lax.scan(step, s0.astype(f32), xs)
    o = jnp.moveaxis(o, 0, 2).reshape(b, h, n, dv)
    return o, s_fin


def gdn_branch(qkv, z, gate_cols, p, s_f, s_b):
    bsz, n, _ = qkv.shape
    qkv = jax.nn.silu(dwconv_centred(qkv, p['conv_a_w'], p['conv_a_b']))
    q, k, v = jnp.split(qkv, [D_QK, 2 * D_QK], axis=-1)
    heads = lambda t, d: t.reshape(bsz, n, H_A, d).transpose(0, 2, 1, 3)
    q = l2norm(heads(q, DK_A))
    k = l2norm(heads(k, DK_A))
    v = heads(v, DV_A)
    alpha_f, beta_f, alpha_b, beta_b = jnp.split(gate_cols.astype(jnp.float32), 4, axis=-1)

    def gates(a, bt, a_log, dt_bias):
        g = -jnp.exp(a_log.astype(jnp.float32)) * jax.nn.softplus(a + dt_bias.astype(jnp.float32))
        return g.transpose(0, 2, 1), jax.nn.sigmoid(bt).transpose(0, 2, 1)

    g_f, bet_f = gates(alpha_f, beta_f, p['a_log_fwd'], p['dt_bias_fwd'])
    g_b, bet_b = gates(alpha_b, beta_b, p['a_log_bwd'], p['dt_bias_bwd'])
    o_f, sf_new = delta_rule_chunked(q, k, v, g_f, bet_f, s_f)
    flip = lambda t: jnp.flip(t, axis=2)
    o_b, sb_new = delta_rule_chunked(flip(q), flip(k), flip(v), flip(g_b), flip(bet_b), s_b)
    o = rmsnorm(o_f + flip(o_b), p['onorm_a_w'])
    o = o.transpose(0, 2, 1, 3).reshape(bsz, n, D_VA).astype(z.dtype)
    return o * jax.nn.silu(z), sf_new, sb_new


def blockdiag(x, w, b):
    bsz, n, _ = x.shape
    y = jnp.einsum('bnhi,hij->bnhj', x.reshape(bsz, n, NB_B, BW_B), w).reshape(bsz, n, D_B)
    return y + b


def rglru(x, wa, ba, wx, bx, lam, h0):
    f32 = jnp.float32
    xf = x.astype(f32)
    r = jax.nn.sigmoid(blockdiag(xf, wa.astype(f32), ba.astype(f32)))
    ig = jax.nn.sigmoid(blockdiag(xf, wx.astype(f32), bx.astype(f32)))
    log_a = -LRU_C * r * jax.nn.softplus(-lam.astype(f32))
    a = jnp.exp(log_a)
    bx_t = jnp.sqrt(-jnp.expm1(2.0 * log_a)) * (ig * xf)
    bx_t = bx_t.at[:, 0].add(a[:, 0] * h0.astype(f32))

    def comb(left, right):
        a_l, b_l = left
        a_r, b_r = right
        return a_l * a_r, a_r * b_l + b_r

    _, hs = lax.associative_scan(comb, (a, bx_t), axis=1)
    return hs, hs[:, -1]


def lru_branch(xb, z, p, h_f, h_b, col_major):
    bsz, n, _ = xb.shape
    if col_major:
        rows = n // GRID_W
        to_order = lambda t: t.reshape(bsz, rows, GRID_W, -1).transpose(0, 2, 1, 3).reshape(bsz, n, -1)
        from_order = lambda t: t.reshape(bsz, GRID_W, rows, -1).transpose(0, 2, 1, 3).reshape(bsz, n, -1)
    else:
        to_order = lambda t: t
        from_order = lambda t: t
    xc = dwconv_centred(to_order(xb), p['conv_b_w'], p['conv_b_b'])
    hf, hf_last = rglru(xc, p['lru_wa_fwd'], p['lru_ba_fwd'], p['lru_wx_fwd'], p['lru_bx_fwd'],
                       p['lru_lambda_fwd'], h_f)
    hb, hb_last = rglru(jnp.flip(xc, axis=1), p['lru_wa_bwd'], p['lru_ba_bwd'], p['lru_wx_bwd'],
                        p['lru_bx_bwd'], p['lru_lambda_bwd'], h_b)
    hsum = from_order(hf + jnp.flip(hb, axis=1)).astype(z.dtype)
    return hsum * jax.nn.silu(z), hf_last, hb_last


def trunk_layer(x, mod, s_af, s_ab, s_bf, s_bb, col_major, p):
    shift, scale, gate = jnp.split(mod, 3, axis=-1)
    h = rmsnorm(x, p['norm_w']) * (1 + scale) + shift
    proj = h @ p['w_in']
    qkv, z_a, x_b, z_b, gate_cols = jnp.split(proj, [OFF_QKV, OFF_ZA, OFF_XB, OFF_ZB], axis=-1)
    o_a, sfa, sba = gdn_branch(qkv, z_a, gate_cols, p, s_af, s_ab)
    o_b, sfb, sbb = lru_branch(x_b, z_b, p, s_bf, s_bb, col_major)
    pa = o_a @ p['w_proj_a']
    pb = o_b @ p['w_proj_b']
    g_a, g_b = jnp.split(jax.nn.sigmoid(h @ p['w_gate'] + p['b_gate']), 2, axis=-1)
    mixed = (g_a * pa + g_b * pb) @ p['w_out']
    return x + gate * mixed, sfa, sba, sfb, sbb


def setup_inputs(seed: int = 0) -> dict:
    key = jax.random.key(seed)
    ks = iter(jax.random.split(key, 64))
    nk = lambda: next(ks)
    nrm = lambda shape, s: jax.random.normal(nk(), shape, jnp.float32) * s
    L = DEPTH
    dt = jnp.exp(jax.random.uniform(nk(), (L, H_A), jnp.float32, np.log(1e-3), np.log(1e-1)))
    dt2 = jnp.exp(jax.random.uniform(nk(), (L, H_A), jnp.float32, np.log(1e-3), np.log(1e-1)))
    inv_sp = lambda d: d + jnp.log(-jnp.expm1(-d))
    a0 = jax.random.uniform(nk(), (L, D_B), jnp.float32, 0.9, 0.999)
    a1 = jax.random.uniform(nk(), (L, D_B), jnp.float32, 0.9, 0.999)
    logit = lambda a: jnp.log(a) - jnp.log1p(-a)
    return {
        'x_prompt': nrm((BATCH, SEQ, D_MODEL), 1.0),
        'x_sample': nrm((DEC_BATCH, DEC_SEQ, D_MODEL), 1.0),
        'state_a_fwd': nrm((DEC_BATCH, DEPTH, H_A, DK_A, DV_A), 0.5),
        'state_a_bwd': nrm((DEC_BATCH, DEPTH, H_A, DK_A, DV_A), 0.5),
        'state_b_fwd': nrm((DEC_BATCH, DEPTH, D_B), 0.5),
        'state_b_bwd': nrm((DEC_BATCH, DEPTH, D_B), 0.5),
        'c': nrm((DEC_BATCH, D_MODEL), 1.0),
        'c_ctx': nrm((D_MODEL,), 1.0),
        'norm_w': 1.0 + nrm((L, D_MODEL), 0.02),
        'w_mod': nrm((L, D_MODEL, 3 * D_MODEL), 0.5 * D_MODEL ** -0.5),
        'b_mod': nrm((L, 3 * D_MODEL), 0.01),
        'w_in': nrm((L, D_MODEL, IN_COLS), D_MODEL ** -0.5),
        'conv_a_w': nrm((L, CONV_K, OFF_QKV), CONV_K ** -0.5),
        'conv_a_b': nrm((L, OFF_QKV), 0.01),
        'a_log_fwd': jnp.log(jax.random.uniform(nk(), (L, H_A), jnp.float32, 1.0, 16.0)),
        'dt_bias_fwd': inv_sp(dt),
        'a_log_bwd': jnp.log(jax.random.uniform(nk(), (L, H_A), jnp.float32, 1.0, 16.0)),
        'dt_bias_bwd': inv_sp(dt2),
        'onorm_a_w': 1.0 + nrm((L, DV_A), 0.02),
        'conv_b_w': nrm((L, CONV_K, D_B), CONV_K ** -0.5),
        'conv_b_b': nrm((L, D_B), 0.01),
        'lru_wa_fwd': nrm((L, NB_B, BW_B, BW_B), BW_B ** -0.5),
        'lru_ba_fwd': nrm((L, D_B), 0.01),
        'lru_wx_fwd': nrm((L, NB_B, BW_B, BW_B), BW_B ** -0.5),
        'lru_bx_fwd': nrm((L, D_B), 0.01),
        'lru_lambda_fwd': logit(a0),
        'lru_wa_bwd': nrm((L, NB_B, BW_B, BW_B), BW_B ** -0.5),
        'lru_ba_bwd': nrm((L, D_B), 0.01),
        'lru_wx_bwd': nrm((L, NB_B, BW_B, BW_B), BW_B ** -0.5),
        'lru_bx_bwd': nrm((L, D_B), 0.01),
        'lru_lambda_bwd': logit(a1),
        'w_proj_a': nrm((L, D_VA, D_MODEL), D_VA ** -0.5),
        'w_proj_b': nrm((L, D_B, D_MODEL), D_B ** -0.5),
        'w_gate': nrm((L, D_MODEL, 2 * D_MODEL), D_MODEL ** -0.5),
        'b_gate': nrm((L, 2 * D_MODEL), 0.01),
        'w_out': nrm((L, D_MODEL, D_MODEL), D_MODEL ** -0.5),
        'final_norm_w': 1.0 + nrm((D_MODEL,), 0.02),
    }


def reference(x_prompt, x_sample, state_a_fwd, state_a_bwd, state_b_fwd, state_b_bwd, c, c_ctx,
              norm_w, w_mod, b_mod, w_in, conv_a_w, conv_a_b, a_log_fwd, dt_bias_fwd, a_log_bwd,
              dt_bias_bwd, onorm_a_w, conv_b_w, conv_b_b, lru_wa_fwd, lru_ba_fwd, lru_wx_fwd,
              lru_bx_fwd, lru_lambda_fwd, lru_wa_bwd, lru_ba_bwd, lru_wx_bwd, lru_bx_bwd,
              lru_lambda_bwd, w_proj_a, w_proj_b, w_gate, b_gate, w_out, final_norm_w):
    bp = x_prompt.shape[0]
    xp, xs = x_prompt, x_sample
    new_af, new_ab, new_bf, new_bb = [], [], [], []
    for l in range(DEPTH):
        p = {
            'norm_w': norm_w[l], 'w_in': w_in[l], 'conv_a_w': conv_a_w[l], 'conv_a_b': conv_a_b[l],
            'a_log_fwd': a_log_fwd[l], 'dt_bias_fwd': dt_bias_fwd[l],
            'a_log_bwd': a_log_bwd[l], 'dt_bias_bwd': dt_bias_bwd[l], 'onorm_a_w': onorm_a_w[l],
            'conv_b_w': conv_b_w[l], 'conv_b_b': conv_b_b[l],
            'lru_wa_fwd': lru_wa_fwd[l], 'lru_ba_fwd': lru_ba_fwd[l], 'lru_wx_fwd': lru_wx_fwd[l],
            'lru_bx_fwd': lru_bx_fwd[l], 'lru_lambda_fwd': lru_lambda_fwd[l],
            'lru_wa_bwd': lru_wa_bwd[l], 'lru_ba_bwd': lru_ba_bwd[l], 'lru_wx_bwd': lru_wx_bwd[l],
            'lru_bx_bwd': lru_bx_bwd[l], 'lru_lambda_bwd': lru_lambda_bwd[l],
            'w_proj_a': w_proj_a[l], 'w_proj_b': w_proj_b[l], 'w_gate': w_gate[l],
            'b_gate': b_gate[l], 'w_out': w_out[l],
        }
        mod_ctx = (jax.nn.silu(c_ctx) @ w_mod[l] + b_mod[l])[None, None, :]
        mod_lat = (jax.nn.silu(c) @ w_mod[l] + b_mod[l])[:, None, :]
        za = jnp.zeros((bp, H_A, DK_A, DV_A), jnp.float32)
        zb = jnp.zeros((bp, D_B), jnp.float32)
        xp, saf, sab, sbf, sbb = trunk_layer(xp, mod_ctx, za, za, zb, zb, False, p)
        new_af.append(saf.astype(x_prompt.dtype))
        new_ab.append(sab.astype(x_prompt.dtype))
        new_bf.append(sbf.astype(x_prompt.dtype))
        new_bb.append(sbb.astype(x_prompt.dtype))
        xs, _, _, _, _ = trunk_layer(xs, mod_lat, state_a_fwd[:, l], state_a_bwd[:, l],
                                     state_b_fwd[:, l], state_b_bwd[:, l], True, p)
    y_prompt = rmsnorm(xp, final_norm_w)
    y_sample = rmsnorm(xs, final_norm_w)
    new_state_a_fwd = jnp.stack(new_af, axis=1)
    new_state_a_bwd = jnp.stack(new_ab, axis=1)
    new_state_b_fwd = jnp.stack(new_bf, axis=1)
    new_state_b_bwd = jnp.stack(new_bb, axis=1)
    return (y_prompt, y_sample, new_state_a_fwd, new_state_a_bwd, new_state_b_fwd, new_state_b_bwd)
```

```python
import functools

import jax
import jax.numpy as jnp
from jax import lax
from jax.experimental import pallas as pl
from jax.experimental.pallas import tpu as pltpu

F32 = jnp.float32
BF16 = jnp.bfloat16
HIGHEST = lax.Precision.HIGHEST

D_MODEL = 1024
N_HEADS = 8
D_HEAD = 128
CHUNK = 64
GRID_W = 64
LRU_C = 8.0
EPS = 1e-6
LANES = 128
SUBLANES = 8
VMEM_LIMIT_BYTES = 56 * 1024 * 1024

TM_A = 256
TM_C = 256
TS_GDN = 256


def _bdot(a, b):
    return jnp.dot(a.astype(BF16), b.astype(BF16), preferred_element_type=F32)


def _sigmoid(x):
    return 1.0 / (1.0 + jnp.exp(-x))


def _softplus(x):
    return jnp.maximum(x, 0.0) + jnp.log1p(jnp.exp(-jnp.abs(x)))


def _neg_expm1(x):
    u = jnp.exp(x)
    small = (1.0 - u) * x / jnp.log(jnp.where(u == 1.0, 0.5, u))
    return jnp.where(x > -0.5, jnp.where(u == 1.0, -x, small), 1.0 - u)


def _const_spec(shape):
    n = len(shape)
    return pl.BlockSpec(shape, lambda *_: (0,) * n)


def _params(sem, vmem=VMEM_LIMIT_BYTES):
    return pltpu.CompilerParams(dimension_semantics=sem, vmem_limit_bytes=vmem)


def _mod_kernel(c_ref, w_ref, b_ref, o_ref):
    c = c_ref[...]
    o_ref[...] = _bdot(c * _sigmoid(c), w_ref[...]) + b_ref[...]


def _mod_call(cvec, w_mod, b_mod):
    n_col = 3
    return pl.pallas_call(
        _mod_kernel,
        grid=(n_col,),
        in_specs=[
            _const_spec((SUBLANES, D_MODEL)),
            pl.BlockSpec((D_MODEL, D_MODEL), lambda j: (0, j)),
            pl.BlockSpec((1, D_MODEL), lambda j: (0, j)),
        ],
        out_specs=pl.BlockSpec((SUBLANES, D_MODEL), lambda j: (0, j)),
        out_shape=jax.ShapeDtypeStruct((SUBLANES, 3 * D_MODEL), F32),
        compiler_params=_params(("arbitrary",)),
        name="mod_vectors",
    )(cvec, w_mod, b_mod)


def _modulated_norm(x, mod_ref, nw_ref, row):
    mod = mod_ref[pl.ds(row, 1), :]
    shift = mod[:, 0:D_MODEL]
    scale = mod[:, D_MODEL:2 * D_MODEL]
    gate = mod[:, 2 * D_MODEL:3 * D_MODEL]
    y = x * lax.rsqrt(jnp.mean(x * x, axis=-1, keepdims=True) + EPS) * nw_ref[...]
    return y * (1.0 + scale) + shift, gate


def _stage_a_kernel(tiles_per_seq, row0, x_ref, mod_ref, nw_ref, wqkv_ref, wza_ref, wxb_ref,
                    wzb_ref, wgc_ref, alog_ref, dtb_ref, qkv_ref, sza_ref, xb_ref, szb_ref, gc_ref):
    row = row0 + pl.program_id(0) // tiles_per_seq
    h, _ = _modulated_norm(x_ref[...], mod_ref, nw_ref, row)
    hb = h.astype(BF16)
    for j in range(3):
        sl = slice(j * D_MODEL, (j + 1) * D_MODEL)
        qkv_ref[:, sl] = jnp.dot(hb, wqkv_ref[:, sl], preferred_element_type=F32)
    za = jnp.dot(hb, wza_ref[...], preferred_element_type=F32)
    sza_ref[...] = za * _sigmoid(za)
    xb_ref[...] = jnp.dot(hb, wxb_ref[...], preferred_element_type=F32)
    zb = jnp.dot(hb, wzb_ref[...], preferred_element_type=F32)
    szb_ref[...] = zb * _sigmoid(zb)
    gc = jnp.dot(hb, wgc_ref[...], preferred_element_type=F32)
    lane = lax.broadcasted_iota(jnp.int32, gc.shape, 1)
    is_decay = (lane < 8) | ((lane >= 16) & (lane < 24))
    log_decay = -jnp.exp(alog_ref[...]) * _softplus(gc + dtb_ref[...])
    gc_ref[...] = jnp.where(is_decay, log_decay, _sigmoid(gc))


def _stage_a_call(x, mod, norm_w, wqkv, wza, wxb, wzb, wgc, alog_vec, dtb_vec, tiles_per_seq, row0):
    n = x.shape[0]
    tm = TM_A
    tok = lambda w: pl.BlockSpec((tm, w), lambda i: (i, 0))
    return pl.pallas_call(
        functools.partial(_stage_a_kernel, tiles_per_seq, row0),
        grid=(n // tm,),
        in_specs=[
            tok(D_MODEL),
            _const_spec((SUBLANES, 3 * D_MODEL)),
            _const_spec((1, D_MODEL)),
            _const_spec((D_MODEL, 3 * D_MODEL)),
            _const_spec((D_MODEL, D_MODEL)),
            _const_spec((D_MODEL, D_MODEL)),
            _const_spec((D_MODEL, D_MODEL)),
            _const_spec((D_MODEL, LANES)),
            _const_spec((1, LANES)),
            _const_spec((1, LANES)),
        ],
        out_specs=[tok(3 * D_MODEL), tok(D_MODEL), tok(D_MODEL), tok(D_MODEL), tok(LANES)],
        out_shape=[
            jax.ShapeDtypeStruct((n, 3 * D_MODEL), F32),
            jax.ShapeDtypeStruct((n, D_MODEL), F32),
            jax.ShapeDtypeStruct((n, D_MODEL), F32),
            jax.ShapeDtypeStruct((n, D_MODEL), F32),
            jax.ShapeDtypeStruct((n, LANES), F32),
        ],
        compiler_params=_params(("arbitrary",)),
        name="stage_a",
    )(x, mod, norm_w, wqkv, wza, wxb, wzb, wgc, alog_vec, dtb_vec)


def _gdn_kernel(reverse, nt, qkv_ref, hp_ref, hn_ref, gc_ref, cw_ref, cb_ref, s0_ref,
                o_ref, sfin_ref, xpad, qs, ks, vs, state):
    ts = TS_GDN
    nc = ts // CHUNK
    t = pl.program_id(1)
    ti = (nt - 1 - t) if reverse else t

    @pl.when(t == 0)
    def _init():
        state[...] = s0_ref[0]

    xpad[SUBLANES:SUBLANES + ts, :] = qkv_ref[...]
    xpad[0:SUBLANES, :] = jnp.where(ti == 0, 0.0, hp_ref[...])
    xpad[SUBLANES + ts:2 * SUBLANES + ts, :] = jnp.where(ti == nt - 1, 0.0, hn_ref[...])

    def conv_block(col):
        cs = slice(col, col + D_HEAD)
        acc = cb_ref[:, cs] + xpad[SUBLANES - 2:SUBLANES - 2 + ts, cs] * cw_ref[0:1, cs]
        for i in range(1, 4):
            acc = acc + xpad[SUBLANES - 2 + i:SUBLANES - 2 + i + ts, cs] * cw_ref[i:i + 1, cs]
        return acc * _sigmoid(acc)

    def l2n(v):
        return v * lax.rsqrt(jnp.sum(v * v, axis=-1, keepdims=True) + EPS)

    for h in range(N_HEADS):
        hs = slice(h * D_HEAD, (h + 1) * D_HEAD)
        qs[:, hs] = l2n(conv_block(h * D_HEAD)) * (D_HEAD ** -0.5)
        ks[:, hs] = l2n(conv_block(D_MODEL + h * D_HEAD))
        vs[:, hs] = conv_block(2 * D_MODEL + h * D_HEAD)

    ri = lax.broadcasted_iota(jnp.int32, (CHUNK, CHUNK), 0)
    ci = lax.broadcasted_iota(jnp.int32, (CHUNK, CHUNK), 1)
    if reverse:
        ri, ci = ci, ri
    incl = ri >= ci
    strict = ri > ci
    eye = ri == ci
    cum_mat = incl.astype(F32)
    ones_mat = jnp.ones((2 * CHUNK, CHUNK), F32)
    level_masks = []
    for lg in range(6):
        s = 1 << lg
        same = (ri >> (lg + 1)) == (ci >> (lg + 1))
        level_masks.append(same & ((ri & (2 * s - 1)) >= s) & ((ci & (2 * s - 1)) < s))
    g_off = 16 if reverse else 0
    b_off = 24 if reverse else 8

    def chunk_body(cidx, carry):
        c = (nc - 1 - cidx) if reverse else cidx
        r0 = pl.multiple_of(c * CHUNK, CHUNK)
        gcol = gc_ref[pl.ds(r0, CHUNK), :]
        g_cum = jnp.dot(cum_mat, gcol, precision=HIGHEST, preferred_element_type=F32)
        g_cum_t = lax.dot_general(gcol, cum_mat, (((0,), (1,)), ((), ())), precision=HIGHEST,
                                  preferred_element_type=F32)
        g_tot = jnp.dot(ones_mat, gcol, precision=HIGHEST, preferred_element_type=F32)
        for h in range(N_HEADS):
            hs = slice(h * D_HEAD, (h + 1) * D_HEAD)
            lg, lb = g_off + h, b_off + h
            gc_i = g_cum[:, lg:lg + 1]
            gc_j = g_cum_t[lg:lg + 1, :]
            beta = gcol[:, lb:lb + 1]
            decay = jnp.exp(jnp.where(incl, gc_i - gc_j, -1e30))
            e_g = jnp.exp(gc_i)
            e_rest = jnp.exp(g_tot[0:CHUNK, lg:lg + 1] - gc_i)
            e_tot = jnp.exp(g_tot[:, lg:lg + 1])
            q = qs[pl.ds(r0, CHUNK), hs]
            k = ks[pl.ds(r0, CHUNK), hs]
            v = vs[pl.ds(r0, CHUNK), hs]
            kb = k * beta
            k_bf = k.astype(BF16)
            prod = lax.dot_general(jnp.concatenate([kb, q], axis=0).astype(BF16), k_bf,
                                   (((1,), (1,)), ((), ())), preferred_element_type=F32)
            a_mat = jnp.where(strict, prod[0:CHUNK] * decay, 0.0)
            qk = prod[CHUNK:2 * CHUNK] * decay
            t_inv = jnp.where(eye, 1.0, 0.0) - jnp.where(level_masks[0], a_mat, 0.0)
            for lvl in range(1, 6):
                a_s = jnp.where(level_masks[lvl], a_mat, 0.0)
                t_inv = t_inv - _bdot(t_inv, _bdot(a_s, t_inv))
            rhs = jnp.concatenate([v * beta, kb * e_g], axis=1)
            sol = rhs + _bdot(jnp.where(eye, 0.0, t_inv), rhs)
            u = sol[:, 0:D_HEAD]
            w = sol[:, D_HEAD:2 * D_HEAD]
            s_h = state[h]
            ws_qs = _bdot(jnp.concatenate([w, q * e_g], axis=0), s_h)
            v_new = u - ws_qs[0:CHUNK]
            v_new_bf = v_new.astype(BF16)
            o_ref[pl.ds(r0, CHUNK), hs] = ws_qs[CHUNK:2 * CHUNK] + jnp.dot(
                qk.astype(BF16), v_new_bf, preferred_element_type=F32)
            kd = (k * e_rest).astype(BF16)
            state[h] = s_h * e_tot + lax.dot_general(
                kd, v_new_bf, (((0,), (0,)), ((), ())), preferred_element_type=F32)
        return carry

    lax.fori_loop(0, nc, chunk_body, 0)

    @pl.when(t == nt - 1)
    def _fin():
        sfin_ref[0] = state[...]


def _gdn_call(qkv, gcols, conv_w, conv_b, s0, batch, reverse):
    n = qkv.shape[0]
    ts = TS_GDN
    nt = n // batch // ts
    blocks8 = ts // SUBLANES

    def tile(b, t):
        return b * nt + ((nt - 1 - t) if reverse else t)

    state_spec = pl.BlockSpec((1, N_HEADS, D_HEAD, D_HEAD), lambda b, t: (b, 0, 0, 0))
    return pl.pallas_call(
        functools.partial(_gdn_kernel, reverse, nt),
        grid=(batch, nt),
        in_specs=[
            pl.BlockSpec((ts, 3 * D_MODEL), lambda b, t: (tile(b, t), 0)),
            pl.BlockSpec((SUBLANES, 3 * D_MODEL),
                         lambda b, t: (jnp.maximum(tile(b, t) * blocks8 - 1, 0), 0)),
            pl.BlockSpec((SUBLANES, 3 * D_MODEL),
                         lambda b, t: (jnp.minimum((tile(b, t) + 1) * blocks8, n // SUBLANES - 1), 0)),
            pl.BlockSpec((ts, LANES), lambda b, t: (tile(b, t), 0)),
            _const_spec((4, 3 * D_MODEL)),
            _const_spec((1, 3 * D_MODEL)),
            state_spec,
        ],
        out_specs=[pl.BlockSpec((ts, D_MODEL), lambda b, t: (tile(b, t), 0)), state_spec],
        out_shape=[
            jax.ShapeDtypeStruct((n, D_MODEL), F32),
            jax.ShapeDtypeStruct((batch, N_HEADS, D_HEAD, D_HEAD), F32),
        ],
        scratch_shapes=[
            pltpu.VMEM((ts + 2 * SUBLANES, 3 * D_MODEL), F32),
            pltpu.VMEM((ts, D_MODEL), F32),
            pltpu.VMEM((ts, D_MODEL), F32),
            pltpu.VMEM((ts, D_MODEL), F32),
            pltpu.VMEM((N_HEADS, D_HEAD, D_HEAD), F32),
        ],
        compiler_params=_params(("arbitrary", "arbitrary")),
        name="gdn_bwd" if reverse else "gdn_fwd",
    )(qkv, qkv, qkv, gcols, conv_w, conv_b, s0)


def _lru_gates(xc, wcat_ref, bcat_ref, lam_ref):
    y = _bdot(xc, wcat_ref[0]) + bcat_ref[0]
    sp = _softplus(-lam_ref[0])
    out = []
    for d in range(2):
        r = _sigmoid(y[:, (2 * d) * LANES:(2 * d + 1) * LANES])
        ig = _sigmoid(y[:, (2 * d + 1) * LANES:(2 * d + 2) * LANES])
        log_a = -LRU_C * r * sp[:, d * LANES:(d + 1) * LANES]
        a = jnp.exp(log_a)
        out.append((a, jnp.sqrt(_neg_expm1(2.0 * log_a)) * (ig * xc)))
    return out


def _lru_ctx_kernel(seq, xb_ref, szb_ref, cw_ref, cb_ref, wcat_ref, bcat_ref, lam_ref,
                    ob_ref, hfl_ref, hbl_ref, a_f, b_f, a_b, b_b):
    nb = xb_ref.shape[0] // seq
    rowi = lax.broadcasted_iota(jnp.int32, (seq, 1), 0)

    def per_seq(b, carry):
        r0 = pl.multiple_of(b * seq, seq)
        x = xb_ref[pl.ds(r0, seq), :]
        xm2 = jnp.where(rowi >= 2, pltpu.roll(x, 2, 0), 0.0)
        xm1 = jnp.where(rowi >= 1, pltpu.roll(x, 1, 0), 0.0)
        xp1 = jnp.where(rowi <= seq - 2, pltpu.roll(x, seq - 1, 0), 0.0)
        xc = (cw_ref[0:1, :] * xm2 + cw_ref[1:2, :] * xm1 + cw_ref[2:3, :] * x
              + cw_ref[3:4, :] * xp1 + cb_ref[...])
        (af, bf), (ab, bb) = _lru_gates(xc, wcat_ref, bcat_ref, lam_ref)
        a_f[pl.ds(r0, seq), :] = af
        b_f[pl.ds(r0, seq), :] = bf
        a_b[pl.ds(r0, seq), :] = ab
        b_b[pl.ds(r0, seq), :] = bb
        return carry

    lax.fori_loop(0, nb, per_seq, 0)

    def step(t, carry):
        hf, hb = carry
        rows_f = pl.ds(t, nb, stride=seq)
        rows_b = pl.ds(seq - 1 - t, nb, stride=seq)
        hf = a_f[rows_f, :] * hf + b_f[rows_f, :]
        hb = a_b[rows_b, :] * hb + b_b[rows_b, :]
        a_f[rows_f, :] = hf
        a_b[rows_b, :] = hb
        return hf, hb

    zero = jnp.zeros((nb, LANES), F32)
    lax.fori_loop(0, seq, step, (zero, zero))

    def write(b, carry):
        r0 = pl.multiple_of(b * seq, seq)
        rows = pl.ds(r0, seq)
        ob_ref[rows, :] = (a_f[rows, :] + a_b[rows, :]) * szb_ref[rows, :]
        return carry

    lax.fori_loop(0, nb, write, 0)
    for b in range(nb):
        hfl_ref[b] = a_f[b * seq + seq - 1:b * seq + seq, :]
        hbl_ref[b] = a_b[b * seq:b * seq + 1, :]


def _lru_lat_kernel(xb_ref, szb_ref, cw_ref, cb_ref, wcat_ref, bcat_ref, lam_ref, h0f_ref, h0b_ref,
                    ob_ref, hfl_ref, hbl_ref, xext, a_f, b_f, a_b, b_b, hin_f, hin_b):
    gw = GRID_W
    n = xb_ref.shape[0]
    nr = n // gw
    wi = lax.broadcasted_iota(jnp.int32, (gw, 1), 0)
    xext[2 * gw:2 * gw + n, :] = xb_ref[...]
    xext[0:gw, :] = jnp.where(wi >= 1, pltpu.roll(xb_ref[(nr - 2) * gw:(nr - 1) * gw, :], 1, 0), 0.0)
    xext[gw:2 * gw, :] = jnp.where(wi >= 1, pltpu.roll(xb_ref[(nr - 1) * gw:nr * gw, :], 1, 0), 0.0)
    xext[2 * gw + n:3 * gw + n, :] = jnp.where(wi <= gw - 2, pltpu.roll(xb_ref[0:gw, :], gw - 1, 0), 0.0)

    rows_per = 256

    def gates(q, carry):
        r0 = pl.multiple_of(q * rows_per, rows_per)
        xc = cb_ref[...] + cw_ref[0:1, :] * xext[pl.ds(r0, rows_per), :]
        for i in range(1, 4):
            xc = xc + cw_ref[i:i + 1, :] * xext[pl.ds(r0 + i * gw, rows_per), :]
        (af, bf), (ab, bb) = _lru_gates(xc, wcat_ref, bcat_ref, lam_ref)
        a_f[pl.ds(r0, rows_per), :] = af
        b_f[pl.ds(r0, rows_per), :] = bf
        a_b[pl.ds(r0, rows_per), :] = ab
        b_b[pl.ds(r0, rows_per), :] = bb
        return carry

    lax.fori_loop(0, n // rows_per, gates, 0)

    ng = gw // SUBLANES

    def col_scan(r, carry):
        acf, bcf, acb, bcb = carry
        rb = nr - 1 - r
        nacf, nbcf, nacb, nbcb = [], [], [], []
        for g in range(ng):
            rows_f = pl.ds(pl.multiple_of(r * gw + g * SUBLANES, SUBLANES), SUBLANES)
            rows_b = pl.ds(pl.multiple_of(rb * gw + g * SUBLANES, SUBLANES), SUBLANES)
            af = a_f[rows_f, :]
            ab = a_b[rows_b, :]
            caf = af * acf[g]
            cbf = af * bcf[g] + b_f[rows_f, :]
            cab = ab * acb[g]
            cbb = ab * bcb[g] + b_b[rows_b, :]
            a_f[rows_f, :] = caf
            b_f[rows_f, :] = cbf
            a_b[rows_b, :] = cab
            b_b[rows_b, :] = cbb
            nacf.append(caf)
            nbcf.append(cbf)
            nacb.append(cab)
            nbcb.append(cbb)
        return tuple(nacf), tuple(nbcf), tuple(nacb), tuple(nbcb)

    one = tuple(jnp.ones((SUBLANES, LANES), F32) for _ in range(ng))
    zero = tuple(jnp.zeros((SUBLANES, LANES), F32) for _ in range(ng))
    lax.fori_loop(0, nr, col_scan, (one, zero, one, zero))

    def carry_cols(w, carry):
        hf, hb = carry
        wb = gw - 1 - w
        hin_f[pl.ds(w, 1), :] = hf
        hin_b[pl.ds(wb, 1), :] = hb
        hf = a_f[pl.ds((nr - 1) * gw + w, 1), :] * hf + b_f[pl.ds((nr - 1) * gw + w, 1), :]
        hb = a_b[pl.ds(wb, 1), :] * hb + b_b[pl.ds(wb, 1), :]
        return hf, hb

    hf, hb = lax.fori_loop(0, gw, carry_cols, (h0f_ref[0], h0b_ref[0]))
    hfl_ref[0] = hf
    hbl_ref[0] = hb

    def finish(r, carry):
        rows = pl.ds(pl.multiple_of(r * gw, gw), gw)
        hsum = (a_f[rows, :] * hin_f[...] + b_f[rows, :]) + (a_b[rows, :] * hin_b[...] + b_b[rows, :])
        ob_ref[rows, :] = hsum * szb_ref[rows, :]
        return carry

    lax.fori_loop(0, nr, finish, 0)


def _lru_common_specs(n, blk):
    col = lambda *ids: ids[-1]
    return [
        pl.BlockSpec((n, LANES), blk),
        pl.BlockSpec((n, LANES), blk),
        pl.BlockSpec((4, LANES), lambda *ids: (0, col(*ids))),
        pl.BlockSpec((1, LANES), lambda *ids: (0, col(*ids))),
        pl.BlockSpec((1, LANES, 4 * LANES), lambda *ids: (col(*ids), 0, 0)),
        pl.BlockSpec((1, 1, 4 * LANES), lambda *ids: (col(*ids), 0, 0)),
        pl.BlockSpec((1, 1, 2 * LANES), lambda *ids: (col(*ids), 0, 0)),
    ]


def _lru_ctx_call(xb, szb, cw, cb, wcat, bcat, lam, batch):
    n = xb.shape[0]
    seq = n // batch
    nblk = D_MODEL // LANES
    st_spec = pl.BlockSpec((batch, 1, LANES), lambda j: (0, 0, j))
    return pl.pallas_call(
        functools.partial(_lru_ctx_kernel, seq),
        grid=(nblk,),
        in_specs=_lru_common_specs(n, lambda j: (0, j)),
        out_specs=[pl.BlockSpec((n, LANES), lambda j: (0, j)), st_spec, st_spec],
        out_shape=[
            jax.ShapeDtypeStruct((n, D_MODEL), F32),
            jax.ShapeDtypeStruct((batch, 1, D_MODEL), F32),
            jax.ShapeDtypeStruct((batch, 1, D_MODEL), F32),
        ],
        scratch_shapes=[pltpu.VMEM((n, LANES), F32) for _ in range(4)],
        compiler_params=_params(("arbitrary",)),
        name="lru_ctx",
    )(xb, szb, cw, cb, wcat, bcat, lam)


def _lru_lat_call(xb, szb, cw, cb, wcat, bcat, lam, h0f, h0b, batch):
    n = xb.shape[0] // batch
    nblk = D_MODEL // LANES
    st_spec = pl.BlockSpec((1, 1, LANES), lambda b, j: (b, 0, j))
    return pl.pallas_call(
        _lru_lat_kernel,
        grid=(batch, nblk),
        in_specs=_lru_common_specs(n, lambda b, j: (b, j)) + [st_spec, st_spec],
        out_specs=[pl.BlockSpec((n, LANES), lambda b, j: (b, j)), st_spec, st_spec],
        out_shape=[
            jax.ShapeDtypeStruct((batch * n, D_MODEL), F32),
            jax.ShapeDtypeStruct((batch, 1, D_MODEL), F32),
            jax.ShapeDtypeStruct((batch, 1, D_MODEL), F32),
        ],
        scratch_shapes=[pltpu.VMEM((n + 3 * GRID_W, LANES), F32)]
        + [pltpu.VMEM((n, LANES), F32) for _ in range(4)]
        + [pltpu.VMEM((GRID_W, LANES), F32) for _ in range(2)],
        compiler_params=_params(("arbitrary", "arbitrary")),
        name="lru_lat",
    )(xb, szb, cw, cb, wcat, bcat, lam, h0f, h0b)


def _stage_c_kernel(tiles_per_seq, row0, x_ref, mod_ref, nw_ref, of_ref, ob_ref, sza_ref, olru_ref,
                    wg_ref, bg_ref, wpa_ref, wpb_ref, wo_ref, onw_ref, fnw_ref, y_ref):
    row = row0 + pl.program_id(0) // tiles_per_seq
    x = x_ref[...]
    h, gate = _modulated_norm(x, mod_ref, nw_ref, row)
    g = _sigmoid(_bdot(h, wg_ref[...]) + bg_ref[...])
    o = of_ref[...] + ob_ref[...]
    heads = []
    for hh in range(N_HEADS):
        oh = o[:, hh * D_HEAD:(hh + 1) * D_HEAD]
        heads.append(oh * lax.rsqrt(jnp.mean(oh * oh, axis=-1, keepdims=True) + EPS) * onw_ref[...])
    o_a = jnp.concatenate(heads, axis=1) * sza_ref[...]
    pa = _bdot(o_a, wpa_ref[...])
    pb = _bdot(olru_ref[...], wpb_ref[...])
    mixed = _bdot(g[:, 0:D_MODEL] * pa + g[:, D_MODEL:2 * D_MODEL] * pb, wo_ref[...])
    r = x + gate * mixed
    y_ref[...] = r * lax.rsqrt(jnp.mean(r * r, axis=-1, keepdims=True) + EPS) * fnw_ref[...]


def _stage_c_call(x, mod, norm_w, o_f, o_b, sza, olru, wg, bg, wpa, wpb, wo, onw, fnw,
                  tiles_per_seq, row0):
    n = x.shape[0]
    tm = TM_C
    tok = pl.BlockSpec((tm, D_MODEL), lambda i: (i, 0))
    return pl.pallas_call(
        functools.partial(_stage_c_kernel, tiles_per_seq, row0),
        grid=(n // tm,),
        in_specs=[
            tok,
            _const_spec((SUBLANES, 3 * D_MODEL)),
            _const_spec((1, D_MODEL)),
            tok, tok, tok, tok,
            _const_spec((D_MODEL, 2 * D_MODEL)),
            _const_spec((1, 2 * D_MODEL)),
            _const_spec((D_MODEL, D_MODEL)),
            _const_spec((D_MODEL, D_MODEL)),
            _const_spec((D_MODEL, D_MODEL)),
            _const_spec((1, D_HEAD)),
            _const_spec((1, D_MODEL)),
        ],
        out_specs=tok,
        out_shape=jax.ShapeDtypeStruct((n, D_MODEL), F32),
        compiler_params=_params(("arbitrary",)),
        name="stage_c",
    )(x, mod, norm_w, o_f, o_b, sza, olru, wg, bg, wpa, wpb, wo, onw, fnw)


def kernel(x_prompt, x_sample, state_a_fwd, state_a_bwd, state_b_fwd, state_b_bwd, c, c_ctx,
           norm_w, w_mod, b_mod, w_in, conv_a_w, conv_a_b, a_log_fwd, dt_bias_fwd, a_log_bwd,
           dt_bias_bwd, onorm_a_w, conv_b_w, conv_b_b, lru_wa_fwd, lru_ba_fwd, lru_wx_fwd,
           lru_bx_fwd, lru_lambda_fwd, lru_wa_bwd, lru_ba_bwd, lru_wx_bwd, lru_bx_bwd,
           lru_lambda_bwd, w_proj_a, w_proj_b, w_gate, b_gate, w_out, final_norm_w):
    bp, seq, d = x_prompt.shape
    bd, dseq, _ = x_sample.shape
    l = 0
    d3 = 3 * D_MODEL

    w_in_l = w_in[l]
    wqkv = w_in_l[:, 0:d3].astype(BF16)
    wza = w_in_l[:, d3:d3 + D_MODEL].astype(BF16)
    wxb = w_in_l[:, d3 + D_MODEL:d3 + 2 * D_MODEL].astype(BF16)
    wzb = w_in_l[:, d3 + 2 * D_MODEL:d3 + 3 * D_MODEL].astype(BF16)
    n_gate_cols = 4 * N_HEADS
    wgc = jnp.pad(w_in_l[:, d3 + 3 * D_MODEL:], ((0, 0), (0, LANES - n_gate_cols))).astype(BF16)
    zeros8 = jnp.zeros((N_HEADS,), F32)
    pad_lanes = jnp.zeros((LANES - n_gate_cols,), F32)
    alog_vec = jnp.concatenate([a_log_fwd[l], zeros8, a_log_bwd[l], zeros8, pad_lanes])[None, :]
    dtb_vec = jnp.concatenate([dt_bias_fwd[l], zeros8, dt_bias_bwd[l], zeros8, pad_lanes])[None, :]
    nw = norm_w[l][None, :]
    cvec = jnp.concatenate([c_ctx[None, :], c, jnp.zeros((SUBLANES - 1 - bd, d), F32)], axis=0)
    conv_aw = conv_a_w[l]
    conv_ab = conv_a_b[l][None, :]
    conv_bw = conv_b_w[l]
    conv_bb = conv_b_b[l][None, :]
    wcat = jnp.concatenate([lru_wa_fwd[l], lru_wx_fwd[l], lru_wa_bwd[l], lru_wx_bwd[l]],
                           axis=2).astype(BF16)
    nblk = D_MODEL // LANES
    blk = lambda v: v.reshape(nblk, 1, LANES)
    bcat = jnp.concatenate([blk(lru_ba_fwd[l]), blk(lru_bx_fwd[l]), blk(lru_ba_bwd[l]),
                            blk(lru_bx_bwd[l])], axis=2)
    lam = jnp.concatenate([blk(lru_lambda_fwd[l]), blk(lru_lambda_bwd[l])], axis=2)
    wg = w_gate[l].astype(BF16)
    bg = b_gate[l][None, :]
    wpa = w_proj_a[l].astype(BF16)
    wpb = w_proj_b[l].astype(BF16)
    wo = w_out[l].astype(BF16)
    onw = onorm_a_w[l][None, :]
    fnw = final_norm_w[None, :]

    mod = _mod_call(cvec, w_mod[l], b_mod[l][None, :])

    def path(x2d, batch, mod_row0, tiles_per_seq_a, tiles_per_seq_c, s_af, s_ab, lru_fn):
        qkv, sza, xb, szb, gcols = _stage_a_call(
            x2d, mod, nw, wqkv, wza, wxb, wzb, wgc, alog_vec, dtb_vec, tiles_per_seq_a, mod_row0)
        o_f, saf = _gdn_call(qkv, gcols, conv_aw, conv_ab, s_af, batch, reverse=False)
        o_b, sab = _gdn_call(qkv, gcols, conv_aw, conv_ab, s_ab, batch, reverse=True)
        olru, sbf, sbb = lru_fn(xb, szb)
        y = _stage_c_call(x2d, mod, nw, o_f, o_b, sza, olru, wg, bg, wpa, wpb, wo, onw, fnw,
                          tiles_per_seq_c, mod_row0)
        return y, saf, sab, sbf, sbb

    n_ctx = bp * seq
    zero_state = jnp.zeros((bp, N_HEADS, D_HEAD, D_HEAD), F32)
    yp, saf, sab, sbf, sbb = path(
        x_prompt.reshape(n_ctx, d), bp, 0, n_ctx // TM_A, n_ctx // TM_C, zero_state, zero_state,
        lambda xb, szb: _lru_ctx_call(xb, szb, conv_bw, conv_bb, wcat, bcat, lam, bp))

    ys, _, _, _, _ = path(
        x_sample.reshape(bd * dseq, d), bd, 1, dseq // TM_A, dseq // TM_C,
        state_a_fwd[:, l], state_a_bwd[:, l],
        lambda xb, szb: _lru_lat_call(xb, szb, conv_bw, conv_bb, wcat, bcat, lam,
                                      state_b_fwd[:, l][:, None, :], state_b_bwd[:, l][:, None, :], bd))

    return (yp.reshape(bp, seq, d), ys.reshape(bd, dseq, d),
            saf[:, None], sab[:, None], sbf, sbb)
```

```python
import functools

import jax
import jax.numpy as jnp
from jax import lax
from jax.experimental import pallas as pl
from jax.experimental.pallas import tpu as pltpu

F32 = jnp.float32
BF16 = jnp.bfloat16
HIGHEST = lax.Precision.HIGHEST

D_MODEL = 1024
N_HEADS = 8
D_HEAD = 128
CHUNK = 64
GRID_W = 64
LRU_C = 8.0
EPS = 1e-6
LANES = 128
SUBLANES = 8
VMEM_LIMIT_BYTES = 56 * 1024 * 1024

TM_A = 256
TM_C = 256
TS_GDN = 256


def _bdot(a, b):
    return jnp.dot(a.astype(BF16), b.astype(BF16), preferred_element_type=F32)


def _sigmoid(x):
    return 1.0 / (1.0 + jnp.exp(-x))


def _softplus(x):
    return jnp.maximum(x, 0.0) + jnp.log1p(jnp.exp(-jnp.abs(x)))


def _neg_expm1(x):
    u = jnp.exp(x)
    small = (1.0 - u) * x / jnp.log(jnp.where(u == 1.0, 0.5, u))
    return jnp.where(x > -0.5, jnp.where(u == 1.0, -x, small), 1.0 - u)


def _const_spec(shape):
    n = len(shape)
    return pl.BlockSpec(shape, lambda *_: (0,) * n)


def _params(sem, vmem=VMEM_LIMIT_BYTES):
    return pltpu.CompilerParams(dimension_semantics=sem, vmem_limit_bytes=vmem)


def _mod_kernel(c_ref, w_ref, b_ref, o_ref):
    c = c_ref[...]
    o_ref[...] = _bdot(c * _sigmoid(c), w_ref[...]) + b_ref[...]


def _mod_call(cvec, w_mod, b_mod):
    n_col = 3
    return pl.pallas_call(
        _mod_kernel,
        grid=(n_col,),
        in_specs=[
            _const_spec((SUBLANES, D_MODEL)),
            pl.BlockSpec((D_MODEL, D_MODEL), lambda j: (0, j)),
            pl.BlockSpec((1, D_MODEL), lambda j: (0, j)),
        ],
        out_specs=pl.BlockSpec((SUBLANES, D_MODEL), lambda j: (0, j)),
        out_shape=jax.ShapeDtypeStruct((SUBLANES, 3 * D_MODEL), F32),
        compiler_params=_params(("arbitrary",)),
        name="mod_vectors",
    )(cvec, w_mod, b_mod)


def _modulated_norm(x, mod_ref, nw_ref, row):
    mod = mod_ref[pl.ds(row, 1), :]
    shift = mod[:, 0:D_MODEL]
    scale = mod[:, D_MODEL:2 * D_MODEL]
    gate = mod[:, 2 * D_MODEL:3 * D_MODEL]
    y = x * lax.rsqrt(jnp.mean(x * x, axis=-1, keepdims=True) + EPS) * nw_ref[...]
    return y * (1.0 + scale) + shift, gate


def _stage_a_kernel(tiles_per_seq, row0, x_ref, mod_ref, nw_ref, wqkv_ref, wza_ref, wxb_ref,
                    wzb_ref, wgc_ref, alog_ref, dtb_ref, qkv_ref, sza_ref, xb_ref, szb_ref, gc_ref):
    row = row0 + pl.program_id(0) // tiles_per_seq
    h, _ = _modulated_norm(x_ref[...], mod_ref, nw_ref, row)
    hb = h.astype(BF16)
    for j in range(3):
        sl = slice(j * D_MODEL, (j + 1) * D_MODEL)
        qkv_ref[:, sl] = jnp.dot(hb, wqkv_ref[:, sl], preferred_element_type=F32)
    za = jnp.dot(hb, wza_ref[...], preferred_element_type=F32)
    sza_ref[...] = za * _sigmoid(za)
    xb_ref[...] = jnp.dot(hb, wxb_ref[...], preferred_element_type=F32)
    zb = jnp.dot(hb, wzb_ref[...], preferred_element_type=F32)
    szb_ref[...] = zb * _sigmoid(zb)
    gc = jnp.dot(hb, wgc_ref[...], preferred_element_type=F32)
    lane = lax.broadcasted_iota(jnp.int32, gc.shape, 1)
    is_decay = (lane < 8) | ((lane >= 16) & (lane < 24))
    log_decay = -jnp.exp(alog_ref[...]) * _softplus(gc + dtb_ref[...])
    gc_ref[...] = jnp.where(is_decay, log_decay, _sigmoid(gc))


def _stage_a_call(x, mod, norm_w, wqkv, wza, wxb, wzb, wgc, alog_vec, dtb_vec, tiles_per_seq, row0):
    n = x.shape[0]
    tm = TM_A
    tok = lambda w: pl.BlockSpec((tm, w), lambda i: (i, 0))
    return pl.pallas_call(
        functools.partial(_stage_a_kernel, tiles_per_seq, row0),
        grid=(n // tm,),
        in_specs=[
            tok(D_MODEL),
            _const_spec((SUBLANES, 3 * D_MODEL)),
            _const_spec((1, D_MODEL)),
            _const_spec((D_MODEL, 3 * D_MODEL)),
            _const_spec((D_MODEL, D_MODEL)),
            _const_spec((D_MODEL, D_MODEL)),
            _const_spec((D_MODEL, D_MODEL)),
            _const_spec((D_MODEL, LANES)),
            _const_spec((1, LANES)),
            _const_spec((1, LANES)),
        ],
        out_specs=[tok(3 * D_MODEL), tok(D_MODEL), tok(D_MODEL), tok(D_MODEL), tok(LANES)],
        out_shape=[
            jax.ShapeDtypeStruct((n, 3 * D_MODEL), F32),
            jax.ShapeDtypeStruct((n, D_MODEL), F32),
            jax.ShapeDtypeStruct((n, D_MODEL), F32),
            jax.ShapeDtypeStruct((n, D_MODEL), F32),
            jax.ShapeDtypeStruct((n, LANES), F32),
        ],
        compiler_params=_params(("arbitrary",)),
        name="stage_a",
    )(x, mod, norm_w, wqkv, wza, wxb, wzb, wgc, alog_vec, dtb_vec)


def _delta_rule_tile(reverse, q_ref, k_ref, v_ref, gc_ref, state, o_ref,
                     u_scr, wq_scr, qk_scr, kd_scr, gt_scr):
    ts = q_ref.shape[0]
    nc = ts // CHUNK
    nh = N_HEADS
    ri = lax.broadcasted_iota(jnp.int32, (CHUNK, CHUNK), 0)
    ci = lax.broadcasted_iota(jnp.int32, (CHUNK, CHUNK), 1)
    if reverse:
        ri, ci = ci, ri
    incl = ri >= ci
    strict = ri > ci
    eye = ri == ci
    cum_mat = incl.astype(F32)
    ones_mat = jnp.ones((2 * CHUNK, CHUNK), F32)
    level_masks = []
    for lg in range(6):
        s = 1 << lg
        same = (ri >> (lg + 1)) == (ci >> (lg + 1))
        level_masks.append(same & ((ri & (2 * s - 1)) >= s) & ((ci & (2 * s - 1)) < s))
    g_off = 16 if reverse else 0
    b_off = 24 if reverse else 8

    probs = [(c, h) for c in range(nc) for h in range(nh)]
    rows = lambda c: slice(c * CHUNK, (c + 1) * CHUNK)
    cols = lambda h: slice(h * D_HEAD, (h + 1) * D_HEAD)
    lane = lambda off, h: slice(off + h, off + h + 1)

    gcol = [gc_ref[rows(c), :] for c in range(nc)]
    g_cum = [jnp.dot(cum_mat, g, precision=HIGHEST, preferred_element_type=F32) for g in gcol]
    g_cum_t = [lax.dot_general(g, cum_mat, (((0,), (1,)), ((), ())), precision=HIGHEST,
                               preferred_element_type=F32) for g in gcol]
    g_tot = [jnp.dot(ones_mat, g, precision=HIGHEST, preferred_element_type=F32) for g in gcol]
    for c in range(nc):
        gt_scr[c] = g_tot[c]
    g_i = [g_cum[c][:, lane(g_off, h)] for c, h in probs]
    g_j = [g_cum_t[c][lane(g_off, h), :] for c, h in probs]
    beta = [gcol[c][:, lane(b_off, h)] for c, h in probs]
    k = [k_ref[rows(c), cols(h)].astype(F32) for c, h in probs]
    q = [q_ref[rows(c), cols(h)].astype(F32) for c, h in probs]
    kb = [a * b for a, b in zip(k, beta)]
    prod = [lax.dot_general(jnp.concatenate([a, b], axis=0).astype(BF16), c_.astype(BF16),
                            (((1,), (1,)), ((), ())), preferred_element_type=F32)
            for a, b, c_ in zip(kb, q, k)]
    decay = [jnp.exp(jnp.where(incl, a - b, -1e30)) for a, b in zip(g_i, g_j)]
    a_mat = [jnp.where(strict, p[0:CHUNK] * d, 0.0) for p, d in zip(prod, decay)]
    for i, (p, d) in enumerate(zip(prod, decay)):
        qk_scr[i] = (p[CHUNK:2 * CHUNK] * d).astype(BF16)
    t_inv = [jnp.where(eye, 1.0, 0.0) - jnp.where(level_masks[0], a, 0.0) for a in a_mat]
    for lvl in range(1, 6):
        x = [_bdot(jnp.where(level_masks[lvl], a, 0.0), t) for a, t in zip(a_mat, t_inv)]
        t_inv = [t - _bdot(t, x_) for t, x_ in zip(t_inv, x)]
    e_g = [jnp.exp(g) for g in g_i]
    v = [v_ref[rows(c), cols(h)].astype(F32) for c, h in probs]
    rhs = [jnp.concatenate([v_ * b, kb_ * e], axis=1)
           for v_, b, kb_, e in zip(v, beta, kb, e_g)]
    sol = [r + _bdot(jnp.where(eye, 0.0, t), r) for r, t in zip(rhs, t_inv)]
    for i, (c, h) in enumerate(probs):
        u_scr[i] = sol[i][:, 0:D_HEAD]
        wq_scr[i] = jnp.concatenate([sol[i][:, D_HEAD:2 * D_HEAD], q[i] * e_g[i]],
                                    axis=0).astype(BF16)
        kd_scr[i] = (k[i] * jnp.exp(g_tot[c][0:CHUNK, lane(g_off, h)] - g_i[i])).astype(BF16)

    for c in (range(nc - 1, -1, -1) if reverse else range(nc)):
        idx = [c * nh + h for h in range(nh)]
        s_old = [state[h] for h in range(nh)]
        ws_qs = [jnp.dot(wq_scr[i], s.astype(BF16), preferred_element_type=F32)
                 for i, s in zip(idx, s_old)]
        v_new = [(u_scr[i] - r[0:CHUNK]).astype(BF16) for i, r in zip(idx, ws_qs)]
        for h in range(nh):
            o_ref[rows(c), cols(h)] = ws_qs[h][CHUNK:2 * CHUNK] + jnp.dot(
                qk_scr[idx[h]], v_new[h], preferred_element_type=F32)
        for h in range(nh):
            e_tot = jnp.exp(gt_scr[c][:, lane(g_off, h)])
            state[h] = s_old[h] * e_tot + lax.dot_general(
                kd_scr[idx[h]], v_new[h], (((0,), (0,)), ((), ())), preferred_element_type=F32)


def _gdn_scratch(ts):
    nprob = (ts // CHUNK) * N_HEADS
    return [
        pltpu.VMEM((N_HEADS, D_HEAD, D_HEAD), F32),
        pltpu.VMEM((nprob, CHUNK, D_HEAD), F32),
        pltpu.VMEM((nprob, 2 * CHUNK, D_HEAD), BF16),
        pltpu.VMEM((nprob, CHUNK, CHUNK), BF16),
        pltpu.VMEM((nprob, CHUNK, D_HEAD), BF16),
        pltpu.VMEM((ts // CHUNK, 2 * CHUNK, LANES), F32),
    ]


def _gdn_fwd_kernel(nt, qkv_ref, hp_ref, hn_ref, gc_ref, cw_ref, cb_ref, s0_ref,
                    o_ref, sfin_ref, qo_ref, ko_ref, vo_ref, xpad, qs, ks, vs, state, *scr):
    ts = TS_GDN
    t = pl.program_id(1)

    @pl.when(t == 0)
    def _init():
        state[...] = s0_ref[0]

    xpad[SUBLANES:SUBLANES + ts, :] = qkv_ref[...]
    xpad[0:SUBLANES, :] = jnp.where(t == 0, 0.0, hp_ref[...])
    xpad[SUBLANES + ts:2 * SUBLANES + ts, :] = jnp.where(t == nt - 1, 0.0, hn_ref[...])

    def conv_block(col):
        cs = slice(col, col + D_HEAD)
        acc = cb_ref[:, cs] + xpad[SUBLANES - 2:SUBLANES - 2 + ts, cs] * cw_ref[0:1, cs]
        for i in range(1, 4):
            acc = acc + xpad[SUBLANES - 2 + i:SUBLANES - 2 + i + ts, cs] * cw_ref[i:i + 1, cs]
        return acc * _sigmoid(acc)

    def l2n(v, scale):
        return v * (lax.rsqrt(jnp.sum(v * v, axis=-1, keepdims=True) + EPS) * scale)

    for h in range(N_HEADS):
        hs = slice(h * D_HEAD, (h + 1) * D_HEAD)
        qh = l2n(conv_block(h * D_HEAD), D_HEAD ** -0.5)
        kh = l2n(conv_block(D_MODEL + h * D_HEAD), 1.0)
        vh = conv_block(2 * D_MODEL + h * D_HEAD)
        qs[:, hs] = qh
        ks[:, hs] = kh
        vs[:, hs] = vh
        qo_ref[:, hs] = qh.astype(BF16)
        ko_ref[:, hs] = kh.astype(BF16)
        vo_ref[:, hs] = vh.astype(BF16)

    _delta_rule_tile(False, qs, ks, vs, gc_ref, state, o_ref, *scr)

    @pl.when(t == nt - 1)
    def _fin():
        sfin_ref[0] = state[...]


def _gdn_bwd_kernel(nt, q_ref, k_ref, v_ref, gc_ref, s0_ref, o_ref, sfin_ref, state, *scr):
    t = pl.program_id(1)

    @pl.when(t == 0)
    def _init():
        state[...] = s0_ref[0]

    _delta_rule_tile(True, q_ref, k_ref, v_ref, gc_ref, state, o_ref, *scr)

    @pl.when(t == nt - 1)
    def _fin():
        sfin_ref[0] = state[...]


def _gdn_fwd_call(qkv, gcols, conv_w, conv_b, s0, batch):
    n = qkv.shape[0]
    ts = TS_GDN
    nt = n // batch // ts
    blocks8 = ts // SUBLANES
    tile = lambda b, t: b * nt + t
    state_spec = pl.BlockSpec((1, N_HEADS, D_HEAD, D_HEAD), lambda b, t: (b, 0, 0, 0))
    tok = pl.BlockSpec((ts, D_MODEL), lambda b, t: (tile(b, t), 0))
    return pl.pallas_call(
        functools.partial(_gdn_fwd_kernel, nt),
        grid=(batch, nt),
        in_specs=[
            pl.BlockSpec((ts, 3 * D_MODEL), lambda b, t: (tile(b, t), 0)),
            pl.BlockSpec((SUBLANES, 3 * D_MODEL),
                         lambda b, t: (jnp.maximum(tile(b, t) * blocks8 - 1, 0), 0)),
            pl.BlockSpec((SUBLANES, 3 * D_MODEL),
                         lambda b, t: (jnp.minimum((tile(b, t) + 1) * blocks8, n // SUBLANES - 1), 0)),
            pl.BlockSpec((ts, LANES), lambda b, t: (tile(b, t), 0)),
            _const_spec((4, 3 * D_MODEL)),
            _const_spec((1, 3 * D_MODEL)),
            state_spec,
        ],
        out_specs=[tok, state_spec, tok, tok, tok],
        out_shape=[
            jax.ShapeDtypeStruct((n, D_MODEL), F32),
            jax.ShapeDtypeStruct((batch, N_HEADS, D_HEAD, D_HEAD), F32),
            jax.ShapeDtypeStruct((n, D_MODEL), BF16),
            jax.ShapeDtypeStruct((n, D_MODEL), BF16),
            jax.ShapeDtypeStruct((n, D_MODEL), BF16),
        ],
        scratch_shapes=[
            pltpu.VMEM((ts + 2 * SUBLANES, 3 * D_MODEL), F32),
            pltpu.VMEM((ts, D_MODEL), F32),
            pltpu.VMEM((ts, D_MODEL), F32),
            pltpu.VMEM((ts, D_MODEL), F32),
        ] + _gdn_scratch(ts),
        compiler_params=_params(("arbitrary", "arbitrary")),
        name="gdn_fwd",
    )(qkv, qkv, qkv, gcols, conv_w, conv_b, s0)


def _gdn_bwd_call(q, k, v, gcols, s0, batch):
    n = q.shape[0]
    ts = TS_GDN
    nt = n // batch // ts
    tile = lambda b, t: b * nt + (nt - 1 - t)
    state_spec = pl.BlockSpec((1, N_HEADS, D_HEAD, D_HEAD), lambda b, t: (b, 0, 0, 0))
    tok = pl.BlockSpec((ts, D_MODEL), lambda b, t: (tile(b, t), 0))
    return pl.pallas_call(
        functools.partial(_gdn_bwd_kernel, nt),
        grid=(batch, nt),
        in_specs=[tok, tok, tok, pl.BlockSpec((ts, LANES), lambda b, t: (tile(b, t), 0)), state_spec],
        out_specs=[tok, state_spec],
        out_shape=[
            jax.ShapeDtypeStruct((n, D_MODEL), F32),
            jax.ShapeDtypeStruct((batch, N_HEADS, D_HEAD, D_HEAD), F32),
        ],
        scratch_shapes=_gdn_scratch(ts),
        compiler_params=_params(("arbitrary", "arbitrary")),
        name="gdn_bwd",
    )(q, k, v, gcols, s0)


def _lru_gates(xc, wcat_ref, bcat_ref, lam_ref):
    y = _bdot(xc, wcat_ref[0]) + bcat_ref[0]
    sp = _softplus(-lam_ref[0])
    out = []
    for d in range(2):
        r = _sigmoid(y[:, (2 * d) * LANES:(2 * d + 1) * LANES])
        ig = _sigmoid(y[:, (2 * d + 1) * LANES:(2 * d + 2) * LANES])
        log_a = -LRU_C * r * sp[:, d * LANES:(d + 1) * LANES]
        a = jnp.exp(log_a)
        out.append((a, jnp.sqrt(_neg_expm1(2.0 * log_a)) * (ig * xc)))
    return out


def _lru_ctx_kernel(seq, xb_ref, szb_ref, cw_ref, cb_ref, wcat_ref, bcat_ref, lam_ref,
                    ob_ref, hfl_ref, hbl_ref, a_f, b_f, a_b, b_b):
    nb = xb_ref.shape[0] // seq
    rowi = lax.broadcasted_iota(jnp.int32, (seq, 1), 0)

    def per_seq(b, carry):
        r0 = pl.multiple_of(b * seq, seq)
        x = xb_ref[pl.ds(r0, seq), :]
        xm2 = jnp.where(rowi >= 2, pltpu.roll(x, 2, 0), 0.0)
        xm1 = jnp.where(rowi >= 1, pltpu.roll(x, 1, 0), 0.0)
        xp1 = jnp.where(rowi <= seq - 2, pltpu.roll(x, seq - 1, 0), 0.0)
        xc = (cw_ref[0:1, :] * xm2 + cw_ref[1:2, :] * xm1 + cw_ref[2:3, :] * x
              + cw_ref[3:4, :] * xp1 + cb_ref[...])
        (af, bf), (ab, bb) = _lru_gates(xc, wcat_ref, bcat_ref, lam_ref)
        a_f[pl.ds(r0, seq), :] = af
        b_f[pl.ds(r0, seq), :] = bf
        a_b[pl.ds(r0, seq), :] = ab
        b_b[pl.ds(r0, seq), :] = bb
        return carry

    lax.fori_loop(0, nb, per_seq, 0)

    def step(t, carry):
        hf, hb = carry
        rows_f = pl.ds(t, nb, stride=seq)
        rows_b = pl.ds(seq - 1 - t, nb, stride=seq)
        hf = a_f[rows_f, :] * hf + b_f[rows_f, :]
        hb = a_b[rows_b, :] * hb + b_b[rows_b, :]
        a_f[rows_f, :] = hf
        a_b[rows_b, :] = hb
        return hf, hb

    zero = jnp.zeros((nb, LANES), F32)
    lax.fori_loop(0, seq, step, (zero, zero))

    def write(b, carry):
        r0 = pl.multiple_of(b * seq, seq)
        rows = pl.ds(r0, seq)
        ob_ref[rows, :] = (a_f[rows, :] + a_b[rows, :]) * szb_ref[rows, :]
        return carry

    lax.fori_loop(0, nb, write, 0)
    for b in range(nb):
        hfl_ref[b] = a_f[b * seq + seq - 1:b * seq + seq, :]
        hbl_ref[b] = a_b[b * seq:b * seq + 1, :]


def _lru_lat_kernel(xb_ref, szb_ref, cw_ref, cb_ref, wcat_ref, bcat_ref, lam_ref, h0f_ref, h0b_ref,
                    ob_ref, hfl_ref, hbl_ref, xext, a_f, b_f, a_b, b_b, hin_f, hin_b):
    gw = GRID_W
    n = xb_ref.shape[0]
    nr = n // gw
    wi = lax.broadcasted_iota(jnp.int32, (gw, 1), 0)
    xext[2 * gw:2 * gw + n, :] = xb_ref[...]
    xext[0:gw, :] = jnp.where(wi >= 1, pltpu.roll(xb_ref[(nr - 2) * gw:(nr - 1) * gw, :], 1, 0), 0.0)
    xext[gw:2 * gw, :] = jnp.where(wi >= 1, pltpu.roll(xb_ref[(nr - 1) * gw:nr * gw, :], 1, 0), 0.0)
    xext[2 * gw + n:3 * gw + n, :] = jnp.where(wi <= gw - 2, pltpu.roll(xb_ref[0:gw, :], gw - 1, 0), 0.0)

    rows_per = 256

    def gates(q, carry):
        r0 = pl.multiple_of(q * rows_per, rows_per)
        xc = cb_ref[...] + cw_ref[0:1, :] * xext[pl.ds(r0, rows_per), :]
        for i in range(1, 4):
            xc = xc + cw_ref[i:i + 1, :] * xext[pl.ds(pl.multiple_of(r0 + i * gw, gw), rows_per), :]
        (af, bf), (ab, bb) = _lru_gates(xc, wcat_ref, bcat_ref, lam_ref)
        a_f[pl.ds(r0, rows_per), :] = af
        b_f[pl.ds(r0, rows_per), :] = bf
        a_b[pl.ds(r0, rows_per), :] = ab
        b_b[pl.ds(r0, rows_per), :] = bb
        return carry

    lax.fori_loop(0, n // rows_per, gates, 0)

    ng = gw // SUBLANES

    def col_scan(r, carry):
        acf, bcf, acb, bcb = carry
        rb = nr - 1 - r
        nacf, nbcf, nacb, nbcb = [], [], [], []
        for g in range(ng):
            rows_f = pl.ds(pl.multiple_of(r * gw + g * SUBLANES, SUBLANES), SUBLANES)
            rows_b = pl.ds(pl.multiple_of(rb * gw + g * SUBLANES, SUBLANES), SUBLANES)
            af = a_f[rows_f, :]
            ab = a_b[rows_b, :]
            caf = af * acf[g]
            cbf = af * bcf[g] + b_f[rows_f, :]
            cab = ab * acb[g]
            cbb = ab * bcb[g] + b_b[rows_b, :]
            a_f[rows_f, :] = caf
            b_f[rows_f, :] = cbf
            a_b[rows_b, :] = cab
            b_b[rows_b, :] = cbb
            nacf.append(caf)
            nbcf.append(cbf)
            nacb.append(cab)
            nbcb.append(cbb)
        return tuple(nacf), tuple(nbcf), tuple(nacb), tuple(nbcb)

    one = tuple(jnp.ones((SUBLANES, LANES), F32) for _ in range(ng))
    zero = tuple(jnp.zeros((SUBLANES, LANES), F32) for _ in range(ng))
    lax.fori_loop(0, nr, col_scan, (one, zero, one, zero))

    def carry_cols(w, carry):
        hf, hb = carry
        wb = gw - 1 - w
        hin_f[pl.ds(w, 1), :] = hf
        hin_b[pl.ds(wb, 1), :] = hb
        hf = a_f[pl.ds((nr - 1) * gw + w, 1), :] * hf + b_f[pl.ds((nr - 1) * gw + w, 1), :]
        hb = a_b[pl.ds(wb, 1), :] * hb + b_b[pl.ds(wb, 1), :]
        return hf, hb

    hf, hb = lax.fori_loop(0, gw, carry_cols, (h0f_ref[0], h0b_ref[0]))
    hfl_ref[0] = hf
    hbl_ref[0] = hb

    def finish(r, carry):
        rows = pl.ds(pl.multiple_of(r * gw, gw), gw)
        hsum = (a_f[rows, :] * hin_f[...] + b_f[rows, :]) + (a_b[rows, :] * hin_b[...] + b_b[rows, :])
        ob_ref[rows, :] = hsum * szb_ref[rows, :]
        return carry

    lax.fori_loop(0, nr, finish, 0)


def _lru_common_specs(n, blk):
    col = lambda *ids: ids[-1]
    return [
        pl.BlockSpec((n, LANES), blk),
        pl.BlockSpec((n, LANES), blk),
        pl.BlockSpec((4, LANES), lambda *ids: (0, col(*ids))),
        pl.BlockSpec((1, LANES), lambda *ids: (0, col(*ids))),
        pl.BlockSpec((1, LANES, 4 * LANES), lambda *ids: (col(*ids), 0, 0)),
        pl.BlockSpec((1, 1, 4 * LANES), lambda *ids: (col(*ids), 0, 0)),
        pl.BlockSpec((1, 1, 2 * LANES), lambda *ids: (col(*ids), 0, 0)),
    ]


def _lru_ctx_call(xb, szb, cw, cb, wcat, bcat, lam, batch):
    n = xb.shape[0]
    seq = n // batch
    nblk = D_MODEL // LANES
    st_spec = pl.BlockSpec((batch, 1, LANES), lambda j: (0, 0, j))
    return pl.pallas_call(
        functools.partial(_lru_ctx_kernel, seq),
        grid=(nblk,),
        in_specs=_lru_common_specs(n, lambda j: (0, j)),
        out_specs=[pl.BlockSpec((n, LANES), lambda j: (0, j)), st_spec, st_spec],
        out_shape=[
            jax.ShapeDtypeStruct((n, D_MODEL), F32),
            jax.ShapeDtypeStruct((batch, 1, D_MODEL), F32),
            jax.ShapeDtypeStruct((batch, 1, D_MODEL), F32),
        ],
        scratch_shapes=[pltpu.VMEM((n, LANES), F32) for _ in range(4)],
        compiler_params=_params(("arbitrary",)),
        name="lru_ctx",
    )(xb, szb, cw, cb, wcat, bcat, lam)


def _lru_lat_call(xb, szb, cw, cb, wcat, bcat, lam, h0f, h0b, batch):
    n = xb.shape[0] // batch
    nblk = D_MODEL // LANES
    st_spec = pl.BlockSpec((1, 1, LANES), lambda b, j: (b, 0, j))
    return pl.pallas_call(
        _lru_lat_kernel,
        grid=(batch, nblk),
        in_specs=_lru_common_specs(n, lambda b, j: (b, j)) + [st_spec, st_spec],
        out_specs=[pl.BlockSpec((n, LANES), lambda b, j: (b, j)), st_spec, st_spec],
        out_shape=[
            jax.ShapeDtypeStruct((batch * n, D_MODEL), F32),
            jax.ShapeDtypeStruct((batch, 1, D_MODEL), F32),
            jax.ShapeDtypeStruct((batch, 1, D_MODEL), F32),
        ],
        scratch_shapes=[pltpu.VMEM((n + 3 * GRID_W, LANES), F32)]
        + [pltpu.VMEM((n, LANES), F32) for _ in range(4)]
        + [pltpu.VMEM((GRID_W, LANES), F32) for _ in range(2)],
        compiler_params=_params(("arbitrary", "arbitrary")),
        name="lru_lat",
    )(xb, szb, cw, cb, wcat, bcat, lam, h0f, h0b)


def _stage_c_kernel(tiles_per_seq, row0, x_ref, mod_ref, nw_ref, of_ref, ob_ref, sza_ref, olru_ref,
                    wg_ref, bg_ref, wpa_ref, wpb_ref, wo_ref, onw_ref, fnw_ref, y_ref):
    row = row0 + pl.program_id(0) // tiles_per_seq
    x = x_ref[...]
    h, gate = _modulated_norm(x, mod_ref, nw_ref, row)
    g = _sigmoid(_bdot(h, wg_ref[...]) + bg_ref[...])
    o = of_ref[...] + ob_ref[...]
    heads = []
    for hh in range(N_HEADS):
        oh = o[:, hh * D_HEAD:(hh + 1) * D_HEAD]
        heads.append(oh * lax.rsqrt(jnp.mean(oh * oh, axis=-1, keepdims=True) + EPS) * onw_ref[...])
    o_a = jnp.concatenate(heads, axis=1) * sza_ref[...]
    pa = _bdot(o_a, wpa_ref[...])
    pb = _bdot(olru_ref[...], wpb_ref[...])
    mixed = _bdot(g[:, 0:D_MODEL] * pa + g[:, D_MODEL:2 * D_MODEL] * pb, wo_ref[...])
    r = x + gate * mixed
    y_ref[...] = r * lax.rsqrt(jnp.mean(r * r, axis=-1, keepdims=True) + EPS) * fnw_ref[...]


def _stage_c_call(x, mod, norm_w, o_f, o_b, sza, olru, wg, bg, wpa, wpb, wo, onw, fnw,
                  tiles_per_seq, row0):
    n = x.shape[0]
    tm = TM_C
    tok = pl.BlockSpec((tm, D_MODEL), lambda i: (i, 0))
    return pl.pallas_call(
        functools.partial(_stage_c_kernel, tiles_per_seq, row0),
        grid=(n // tm,),
        in_specs=[
            tok,
            _const_spec((SUBLANES, 3 * D_MODEL)),
            _const_spec((1, D_MODEL)),
            tok, tok, tok, tok,
            _const_spec((D_MODEL, 2 * D_MODEL)),
            _const_spec((1, 2 * D_MODEL)),
            _const_spec((D_MODEL, D_MODEL)),
            _const_spec((D_MODEL, D_MODEL)),
            _const_spec((D_MODEL, D_MODEL)),
            _const_spec((1, D_HEAD)),
            _const_spec((1, D_MODEL)),
        ],
        out_specs=tok,
        out_shape=jax.ShapeDtypeStruct((n, D_MODEL), F32),
        compiler_params=_params(("arbitrary",)),
        name="stage_c",
    )(x, mod, norm_w, o_f, o_b, sza, olru, wg, bg, wpa, wpb, wo, onw, fnw)


def kernel(x_prompt, x_sample, state_a_fwd, state_a_bwd, state_b_fwd, state_b_bwd, c, c_ctx,
           norm_w, w_mod, b_mod, w_in, conv_a_w, conv_a_b, a_log_fwd, dt_bias_fwd, a_log_bwd,
           dt_bias_bwd, onorm_a_w, conv_b_w, conv_b_b, lru_wa_fwd, lru_ba_fwd, lru_wx_fwd,
           lru_bx_fwd, lru_lambda_fwd, lru_wa_bwd, lru_ba_bwd, lru_wx_bwd, lru_bx_bwd,
           lru_lambda_bwd, w_proj_a, w_proj_b, w_gate, b_gate, w_out, final_norm_w):
    bp, seq, d = x_prompt.shape
    bd, dseq, _ = x_sample.shape
    l = 0
    d3 = 3 * D_MODEL

    w_in_l = w_in[l]
    wqkv = w_in_l[:, 0:d3].astype(BF16)
    wza = w_in_l[:, d3:d3 + D_MODEL].astype(BF16)
    wxb = w_in_l[:, d3 + D_MODEL:d3 + 2 * D_MODEL].astype(BF16)
    wzb = w_in_l[:, d3 + 2 * D_MODEL:d3 + 3 * D_MODEL].astype(BF16)
    n_gate_cols = 4 * N_HEADS
    wgc = jnp.pad(w_in_l[:, d3 + 3 * D_MODEL:], ((0, 0), (0, LANES - n_gate_cols))).astype(BF16)
    zeros8 = jnp.zeros((N_HEADS,), F32)
    pad_lanes = jnp.zeros((LANES - n_gate_cols,), F32)
    alog_vec = jnp.concatenate([a_log_fwd[l], zeros8, a_log_bwd[l], zeros8, pad_lanes])[None, :]
    dtb_vec = jnp.concatenate([dt_bias_fwd[l], zeros8, dt_bias_bwd[l], zeros8, pad_lanes])[None, :]
    nw = norm_w[l][None, :]
    cvec = jnp.concatenate([c_ctx[None, :], c, jnp.zeros((SUBLANES - 1 - bd, d), F32)], axis=0)
    conv_aw = conv_a_w[l]
    conv_ab = conv_a_b[l][None, :]
    conv_bw = conv_b_w[l]
    conv_bb = conv_b_b[l][None, :]
    wcat = jnp.concatenate([lru_wa_fwd[l], lru_wx_fwd[l], lru_wa_bwd[l], lru_wx_bwd[l]],
                           axis=2).astype(BF16)
    nblk = D_MODEL // LANES
    blk = lambda v: v.reshape(nblk, 1, LANES)
    bcat = jnp.concatenate([blk(lru_ba_fwd[l]), blk(lru_bx_fwd[l]), blk(lru_ba_bwd[l]),
                            blk(lru_bx_bwd[l])], axis=2)
    lam = jnp.concatenate([blk(lru_lambda_fwd[l]), blk(lru_lambda_bwd[l])], axis=2)
    wg = w_gate[l].astype(BF16)
    bg = b_gate[l][None, :]
    wpa = w_proj_a[l].astype(BF16)
    wpb = w_proj_b[l].astype(BF16)
    wo = w_out[l].astype(BF16)
    onw = onorm_a_w[l][None, :]
    fnw = final_norm_w[None, :]

    mod = _mod_call(cvec, w_mod[l], b_mod[l][None, :])

    def path(x2d, batch, mod_row0, tiles_per_seq_a, tiles_per_seq_c, s_af, s_ab, lru_fn):
        qkv, sza, xb, szb, gcols = _stage_a_call(
            x2d, mod, nw, wqkv, wza, wxb, wzb, wgc, alog_vec, dtb_vec, tiles_per_seq_a, mod_row0)
        o_f, saf, qn, kn, vn = _gdn_fwd_call(qkv, gcols, conv_aw, conv_ab, s_af, batch)
        o_b, sab = _gdn_bwd_call(qn, kn, vn, gcols, s_ab, batch)
        olru, sbf, sbb = lru_fn(xb, szb)
        y = _stage_c_call(x2d, mod, nw, o_f, o_b, sza, olru, wg, bg, wpa, wpb, wo, onw, fnw,
                          tiles_per_seq_c, mod_row0)
        return y, saf, sab, sbf, sbb

    n_ctx = bp * seq
    zero_state = jnp.zeros((bp, N_HEADS, D_HEAD, D_HEAD), F32)
    yp, saf, sab, sbf, sbb = path(
        x_prompt.reshape(n_ctx, d), bp, 0, n_ctx // TM_A, n_ctx // TM_C, zero_state, zero_state,
        lambda xb, szb: _lru_ctx_call(xb, szb, conv_bw, conv_bb, wcat, bcat, lam, bp))

    ys, _, _, _, _ = path(
        x_sample.reshape(bd * dseq, d), bd, 1, dseq // TM_A, dseq // TM_C,
        state_a_fwd[:, l], state_a_bwd[:, l],
        lambda xb, szb: _lru_lat_call(xb, szb, conv_bw, conv_bb, wcat, bcat, lam,
                                      state_b_fwd[:, l][:, None, :], state_b_bwd[:, l][:, None, :], bd))

    return (yp.reshape(bp, seq, d), ys.reshape(bd, dseq, d),
            saf[:, None], sab[:, None], sbf, sbb)
```

```python
import functools

import jax
import jax.numpy as jnp
from jax import lax
from jax.experimental import pallas as pl
from jax.experimental.pallas import tpu as pltpu

F32 = jnp.float32
BF16 = jnp.bfloat16
HIGHEST = lax.Precision.HIGHEST

D_MODEL = 1024
N_HEADS = 8
D_HEAD = 128
CHUNK = 64
GRID_W = 64
LRU_C = 8.0
EPS = 1e-6
LANES = 128
SUBLANES = 8
VMEM_LIMIT_BYTES = 56 * 1024 * 1024

TM_A = 256
TM_C = 256
TS_GDN = 256


def _bdot(a, b):
    return jnp.dot(a.astype(BF16), b.astype(BF16), preferred_element_type=F32)


def _sigmoid(x):
    return 1.0 / (1.0 + jnp.exp(-x))


def _softplus(x):
    return jnp.maximum(x, 0.0) + jnp.log1p(jnp.exp(-jnp.abs(x)))


def _sigmoid_t(x):
    return 0.5 * jnp.tanh(0.5 * x) + 0.5


def _const_spec(shape):
    n = len(shape)
    return pl.BlockSpec(shape, lambda *_: (0,) * n)


def _params(sem, vmem=VMEM_LIMIT_BYTES):
    return pltpu.CompilerParams(dimension_semantics=sem, vmem_limit_bytes=vmem)


def _mod_kernel(c_ref, w_ref, b_ref, o_ref):
    c = c_ref[...]
    o_ref[...] = _bdot(c * _sigmoid(c), w_ref[...]) + b_ref[...]


def _mod_call(cvec, w_mod, b_mod):
    n_col = 3
    return pl.pallas_call(
        _mod_kernel,
        grid=(n_col,),
        in_specs=[
            _const_spec((SUBLANES, D_MODEL)),
            pl.BlockSpec((D_MODEL, D_MODEL), lambda j: (0, j)),
            pl.BlockSpec((1, D_MODEL), lambda j: (0, j)),
        ],
        out_specs=pl.BlockSpec((SUBLANES, D_MODEL), lambda j: (0, j)),
        out_shape=jax.ShapeDtypeStruct((SUBLANES, 3 * D_MODEL), F32),
        compiler_params=_params(("arbitrary",)),
        name="mod_vectors",
    )(cvec, w_mod, b_mod)


def _modulated_norm(x, mod_ref, nw_ref, row):
    mod = mod_ref[pl.ds(row, 1), :]
    shift = mod[:, 0:D_MODEL]
    scale = mod[:, D_MODEL:2 * D_MODEL]
    gate = mod[:, 2 * D_MODEL:3 * D_MODEL]
    y = x * lax.rsqrt(jnp.mean(x * x, axis=-1, keepdims=True) + EPS) * nw_ref[...]
    return y * (1.0 + scale) + shift, gate


def _stage_a_kernel(tiles_per_seq, row0, x_ref, mod_ref, nw_ref, wqkv_ref, wza_ref, wxb_ref,
                    wzb_ref, wgc_ref, alog_ref, dtb_ref, qkv_ref, sza_ref, xb_ref, szb_ref, gc_ref):
    row = row0 + pl.program_id(0) // tiles_per_seq
    h, _ = _modulated_norm(x_ref[...], mod_ref, nw_ref, row)
    hb = h.astype(BF16)
    for j in range(3):
        sl = slice(j * D_MODEL, (j + 1) * D_MODEL)
        qkv_ref[:, sl] = jnp.dot(hb, wqkv_ref[:, sl], preferred_element_type=F32)
    za = jnp.dot(hb, wza_ref[...], preferred_element_type=F32)
    sza_ref[...] = za * _sigmoid(za)
    xb_ref[...] = jnp.dot(hb, wxb_ref[...], preferred_element_type=F32)
    zb = jnp.dot(hb, wzb_ref[...], preferred_element_type=F32)
    szb_ref[...] = zb * _sigmoid(zb)
    gc = jnp.dot(hb, wgc_ref[...], preferred_element_type=F32)
    lane = lax.broadcasted_iota(jnp.int32, gc.shape, 1)
    is_decay = (lane < 8) | ((lane >= 16) & (lane < 24))
    log_decay = -jnp.exp(alog_ref[...]) * _softplus(gc + dtb_ref[...])
    gc_ref[...] = jnp.where(is_decay, log_decay, _sigmoid(gc))


def _stage_a_call(x, mod, norm_w, wqkv, wza, wxb, wzb, wgc, alog_vec, dtb_vec, tiles_per_seq, row0):
    n = x.shape[0]
    tm = TM_A
    tok = lambda w: pl.BlockSpec((tm, w), lambda i: (i, 0))
    return pl.pallas_call(
        functools.partial(_stage_a_kernel, tiles_per_seq, row0),
        grid=(n // tm,),
        in_specs=[
            tok(D_MODEL),
            _const_spec((SUBLANES, 3 * D_MODEL)),
            _const_spec((1, D_MODEL)),
            _const_spec((D_MODEL, 3 * D_MODEL)),
            _const_spec((D_MODEL, D_MODEL)),
            _const_spec((D_MODEL, D_MODEL)),
            _const_spec((D_MODEL, D_MODEL)),
            _const_spec((D_MODEL, LANES)),
            _const_spec((1, LANES)),
            _const_spec((1, LANES)),
        ],
        out_specs=[tok(3 * D_MODEL), tok(D_MODEL), tok(D_MODEL), tok(D_MODEL), tok(LANES)],
        out_shape=[
            jax.ShapeDtypeStruct((n, 3 * D_MODEL), F32),
            jax.ShapeDtypeStruct((n, D_MODEL), F32),
            jax.ShapeDtypeStruct((n, D_MODEL), F32),
            jax.ShapeDtypeStruct((n, D_MODEL), F32),
            jax.ShapeDtypeStruct((n, LANES), F32),
        ],
        compiler_params=_params(("arbitrary",)),
        name="stage_a",
    )(x, mod, norm_w, wqkv, wza, wxb, wzb, wgc, alog_vec, dtb_vec)


def _chunk_cumsum(x, reverse):
    n = x.shape[0]
    row = lax.broadcasted_iota(jnp.int32, x.shape, 0)
    s = 1
    while s < n:
        if reverse:
            x = x + jnp.where(row < n - s, pltpu.roll(x, n - s, 0), 0.0)
        else:
            x = x + jnp.where(row >= s, pltpu.roll(x, s, 0), 0.0)
        s *= 2
    return x


def _delta_rule_tile(reverse, q_ref, k_ref, v_ref, gc_ref, state, o_ref,
                     u_scr, wq_scr, qk_scr, kd_scr, gt_scr):
    ts = q_ref.shape[0]
    nc = ts // CHUNK
    nh = N_HEADS
    ri = lax.broadcasted_iota(jnp.int32, (CHUNK, CHUNK), 0)
    ci = lax.broadcasted_iota(jnp.int32, (CHUNK, CHUNK), 1)
    if reverse:
        ri, ci = ci, ri
    incl = ri >= ci
    strict = ri > ci
    eye = ri == ci
    level_masks = []
    for lg in range(6):
        s = 1 << lg
        same = (ri >> (lg + 1)) == (ci >> (lg + 1))
        level_masks.append(same & ((ri & (2 * s - 1)) >= s) & ((ci & (2 * s - 1)) < s))
    g_off = 16 if reverse else 0
    b_off = 24 if reverse else 8

    probs = [(c, h) for c in range(nc) for h in range(nh)]
    rows = lambda c: slice(c * CHUNK, (c + 1) * CHUNK)
    cols = lambda h: slice(h * D_HEAD, (h + 1) * D_HEAD)
    lane = lambda off, h: slice(off + h, off + h + 1)

    gcol = [gc_ref[rows(c), :] for c in range(nc)]
    g_cum = [_chunk_cumsum(g, reverse) for g in gcol]
    g_cum_t = [g.T for g in g_cum]
    last = 0 if reverse else CHUNK - 1
    g_tot = [g[last:last + 1, :] for g in g_cum]
    for c in range(nc):
        gt_scr[c] = jnp.broadcast_to(g_tot[c], (SUBLANES, LANES))
    g_i = [g_cum[c][:, lane(g_off, h)] for c, h in probs]
    g_j = [g_cum_t[c][lane(g_off, h), :] for c, h in probs]
    beta = [gcol[c][:, lane(b_off, h)] for c, h in probs]
    k = [k_ref[rows(c), cols(h)].astype(F32) for c, h in probs]
    q = [q_ref[rows(c), cols(h)].astype(F32) for c, h in probs]
    kb = [a * b for a, b in zip(k, beta)]
    prod = [lax.dot_general(jnp.concatenate([a, b], axis=0).astype(BF16), c_.astype(BF16),
                            (((1,), (1,)), ((), ())), preferred_element_type=F32)
            for a, b, c_ in zip(kb, q, k)]
    decay = [jnp.exp(jnp.where(incl, a - b, -1e30)) for a, b in zip(g_i, g_j)]
    a_mat = [jnp.where(strict, p[0:CHUNK] * d, 0.0) for p, d in zip(prod, decay)]
    for i, (p, d) in enumerate(zip(prod, decay)):
        qk_scr[i] = (p[CHUNK:2 * CHUNK] * d).astype(BF16)
    t_inv = [jnp.where(eye, 1.0, 0.0) - jnp.where(level_masks[0], a, 0.0) for a in a_mat]
    for lvl in range(1, 6):
        x = [_bdot(jnp.where(level_masks[lvl], a, 0.0), t) for a, t in zip(a_mat, t_inv)]
        t_inv = [t - _bdot(t, x_) for t, x_ in zip(t_inv, x)]
    e_g = [jnp.exp(g) for g in g_i]
    v = [v_ref[rows(c), cols(h)].astype(F32) for c, h in probs]
    rhs = [jnp.concatenate([v_ * b, kb_ * e], axis=1)
           for v_, b, kb_, e in zip(v, beta, kb, e_g)]
    sol = [r + _bdot(jnp.where(eye, 0.0, t), r) for r, t in zip(rhs, t_inv)]
    for i, (c, h) in enumerate(probs):
        u_scr[i] = sol[i][:, 0:D_HEAD]
        wq_scr[i] = jnp.concatenate([sol[i][:, D_HEAD:2 * D_HEAD], q[i] * e_g[i]],
                                    axis=0).astype(BF16)
        kd_scr[i] = (k[i] * jnp.exp(g_tot[c][:, lane(g_off, h)] - g_i[i])).astype(BF16)

    for c in (range(nc - 1, -1, -1) if reverse else range(nc)):
        idx = [c * nh + h for h in range(nh)]
        s_old = [state[h] for h in range(nh)]
        ws_qs = [jnp.dot(wq_scr[i], s.astype(BF16), preferred_element_type=F32)
                 for i, s in zip(idx, s_old)]
        v_new = [(u_scr[i] - r[0:CHUNK]).astype(BF16) for i, r in zip(idx, ws_qs)]
        for h in range(nh):
            o_ref[rows(c), cols(h)] = ws_qs[h][CHUNK:2 * CHUNK] + jnp.dot(
                qk_scr[idx[h]], v_new[h], preferred_element_type=F32)
        for h in range(nh):
            e_tot = jnp.exp(gt_scr[c][0:1, lane(g_off, h)])
            state[h] = s_old[h] * e_tot + lax.dot_general(
                kd_scr[idx[h]], v_new[h], (((0,), (0,)), ((), ())), preferred_element_type=F32)


def _gdn_scratch(ts):
    nprob = (ts // CHUNK) * N_HEADS
    return [
        pltpu.VMEM((N_HEADS, D_HEAD, D_HEAD), F32),
        pltpu.VMEM((nprob, CHUNK, D_HEAD), F32),
        pltpu.VMEM((nprob, 2 * CHUNK, D_HEAD), BF16),
        pltpu.VMEM((nprob, CHUNK, CHUNK), BF16),
        pltpu.VMEM((nprob, CHUNK, D_HEAD), BF16),
        pltpu.VMEM((ts // CHUNK, SUBLANES, LANES), F32),
    ]


def _gdn_fwd_kernel(nt, qkv_ref, hp_ref, hn_ref, gc_ref, cw_ref, cb_ref, s0_ref,
                    o_ref, sfin_ref, qo_ref, ko_ref, vo_ref, xpad, qs, ks, vs, state, *scr):
    ts = TS_GDN
    t = pl.program_id(1)

    @pl.when(t == 0)
    def _init():
        state[...] = s0_ref[0]

    xpad[SUBLANES:SUBLANES + ts, :] = qkv_ref[...]
    xpad[0:SUBLANES, :] = jnp.where(t == 0, 0.0, hp_ref[...])
    xpad[SUBLANES + ts:2 * SUBLANES + ts, :] = jnp.where(t == nt - 1, 0.0, hn_ref[...])

    def conv_block(col):
        cs = slice(col, col + D_HEAD)
        acc = cb_ref[:, cs] + xpad[SUBLANES - 2:SUBLANES - 2 + ts, cs] * cw_ref[0:1, cs]
        for i in range(1, 4):
            acc = acc + xpad[SUBLANES - 2 + i:SUBLANES - 2 + i + ts, cs] * cw_ref[i:i + 1, cs]
        return acc * _sigmoid(acc)

    def l2n(v, scale):
        return v * (lax.rsqrt(jnp.sum(v * v, axis=-1, keepdims=True) + EPS) * scale)

    for h in range(N_HEADS):
        hs = slice(h * D_HEAD, (h + 1) * D_HEAD)
        qh = l2n(conv_block(h * D_HEAD), D_HEAD ** -0.5)
        kh = l2n(conv_block(D_MODEL + h * D_HEAD), 1.0)
        vh = conv_block(2 * D_MODEL + h * D_HEAD)
        qs[:, hs] = qh
        ks[:, hs] = kh
        vs[:, hs] = vh
        qo_ref[:, hs] = qh.astype(BF16)
        ko_ref[:, hs] = kh.astype(BF16)
        vo_ref[:, hs] = vh.astype(BF16)

    _delta_rule_tile(False, qs, ks, vs, gc_ref, state, o_ref, *scr)

    @pl.when(t == nt - 1)
    def _fin():
        sfin_ref[0] = state[...]


def _gdn_bwd_kernel(nt, q_ref, k_ref, v_ref, gc_ref, s0_ref, o_ref, sfin_ref, state, *scr):
    t = pl.program_id(1)

    @pl.when(t == 0)
    def _init():
        state[...] = s0_ref[0]

    _delta_rule_tile(True, q_ref, k_ref, v_ref, gc_ref, state, o_ref, *scr)

    @pl.when(t == nt - 1)
    def _fin():
        sfin_ref[0] = state[...]


def _gdn_fwd_call(qkv, gcols, conv_w, conv_b, s0, batch):
    n = qkv.shape[0]
    ts = TS_GDN
    nt = n // batch // ts
    blocks8 = ts // SUBLANES
    tile = lambda b, t: b * nt + t
    state_spec = pl.BlockSpec((1, N_HEADS, D_HEAD, D_HEAD), lambda b, t: (b, 0, 0, 0))
    tok = pl.BlockSpec((ts, D_MODEL), lambda b, t: (tile(b, t), 0))
    return pl.pallas_call(
        functools.partial(_gdn_fwd_kernel, nt),
        grid=(batch, nt),
        in_specs=[
            pl.BlockSpec((ts, 3 * D_MODEL), lambda b, t: (tile(b, t), 0)),
            pl.BlockSpec((SUBLANES, 3 * D_MODEL),
                         lambda b, t: (jnp.maximum(tile(b, t) * blocks8 - 1, 0), 0)),
            pl.BlockSpec((SUBLANES, 3 * D_MODEL),
                         lambda b, t: (jnp.minimum((tile(b, t) + 1) * blocks8, n // SUBLANES - 1), 0)),
            pl.BlockSpec((ts, LANES), lambda b, t: (tile(b, t), 0)),
            _const_spec((4, 3 * D_MODEL)),
            _const_spec((1, 3 * D_MODEL)),
            state_spec,
        ],
        out_specs=[tok, state_spec, tok, tok, tok],
        out_shape=[
            jax.ShapeDtypeStruct((n, D_MODEL), F32),
            jax.ShapeDtypeStruct((batch, N_HEADS, D_HEAD, D_HEAD), F32),
            jax.ShapeDtypeStruct((n, D_MODEL), BF16),
            jax.ShapeDtypeStruct((n, D_MODEL), BF16),
            jax.ShapeDtypeStruct((n, D_MODEL), BF16),
        ],
        scratch_shapes=[
            pltpu.VMEM((ts + 2 * SUBLANES, 3 * D_MODEL), F32),
            pltpu.VMEM((ts, D_MODEL), F32),
            pltpu.VMEM((ts, D_MODEL), F32),
            pltpu.VMEM((ts, D_MODEL), F32),
        ] + _gdn_scratch(ts),
        compiler_params=_params(("arbitrary", "arbitrary")),
        name="gdn_fwd",
    )(qkv, qkv, qkv, gcols, conv_w, conv_b, s0)


def _gdn_bwd_call(q, k, v, gcols, s0, batch):
    n = q.shape[0]
    ts = TS_GDN
    nt = n // batch // ts
    tile = lambda b, t: b * nt + (nt - 1 - t)
    state_spec = pl.BlockSpec((1, N_HEADS, D_HEAD, D_HEAD), lambda b, t: (b, 0, 0, 0))
    tok = pl.BlockSpec((ts, D_MODEL), lambda b, t: (tile(b, t), 0))
    return pl.pallas_call(
        functools.partial(_gdn_bwd_kernel, nt),
        grid=(batch, nt),
        in_specs=[tok, tok, tok, pl.BlockSpec((ts, LANES), lambda b, t: (tile(b, t), 0)), state_spec],
        out_specs=[tok, state_spec],
        out_shape=[
            jax.ShapeDtypeStruct((n, D_MODEL), F32),
            jax.ShapeDtypeStruct((batch, N_HEADS, D_HEAD, D_HEAD), F32),
        ],
        scratch_shapes=_gdn_scratch(ts),
        compiler_params=_params(("arbitrary", "arbitrary")),
        name="gdn_bwd",
    )(q, k, v, gcols, s0)


def _lru_gates(xc, wcat_ref, bcat_ref, lam_ref):
    th = jnp.tanh(_bdot(xc, wcat_ref[0]) + bcat_ref[0])
    half_c_sp = (-0.5 * LRU_C) * _softplus(-lam_ref[0])
    half_x = 0.5 * xc
    out = []
    for d in range(2):
        th_r = th[:, (2 * d) * LANES:(2 * d + 1) * LANES]
        th_i = th[:, (2 * d + 1) * LANES:(2 * d + 2) * LANES]
        hcs = half_c_sp[:, d * LANES:(d + 1) * LANES]
        log_a = th_r * hcs + hcs
        a = jnp.exp(log_a)
        one_m_a2 = jnp.tanh(log_a) * (-1.0 - a * a)
        out.append((a, jnp.sqrt(one_m_a2) * ((th_i + 1.0) * half_x)))
    return out


def _ctx_pitch(seq):
    tiles = seq // SUBLANES
    return (tiles + 1 - tiles % 2) * SUBLANES


def _lru_ctx_kernel(seq, xb_ref, szb_ref, cw_ref, cb_ref, wcat_ref, bcat_ref, lam_ref,
                    ob_ref, hfl_ref, hbl_ref, a_f, b_f, a_b, b_b):
    nb = xb_ref.shape[0] // seq
    pitch = _ctx_pitch(seq)
    rowi = lax.broadcasted_iota(jnp.int32, (seq, 1), 0)

    def per_seq(b, carry):
        x = xb_ref[pl.ds(pl.multiple_of(b * seq, seq), seq), :]
        r0 = pl.multiple_of(b * pitch, SUBLANES)
        xm2 = jnp.where(rowi >= 2, pltpu.roll(x, 2, 0), 0.0)
        xm1 = jnp.where(rowi >= 1, pltpu.roll(x, 1, 0), 0.0)
        xp1 = jnp.where(rowi <= seq - 2, pltpu.roll(x, seq - 1, 0), 0.0)
        xc = (cw_ref[0:1, :] * xm2 + cw_ref[1:2, :] * xm1 + cw_ref[2:3, :] * x
              + cw_ref[3:4, :] * xp1 + cb_ref[...])
        (af, bf), (ab, bb) = _lru_gates(xc, wcat_ref, bcat_ref, lam_ref)
        a_f[pl.ds(r0, seq), :] = af
        b_f[pl.ds(r0, seq), :] = bf
        a_b[pl.ds(r0, seq), :] = ab
        b_b[pl.ds(r0, seq), :] = bb
        return carry

    lax.fori_loop(0, nb, per_seq, 0)

    def step(t, carry):
        hf, hb = carry
        rows_f = pl.ds(t, nb, stride=pitch)
        rows_b = pl.ds(seq - 1 - t, nb, stride=pitch)
        hf = a_f[rows_f, :] * hf + b_f[rows_f, :]
        hb = a_b[rows_b, :] * hb + b_b[rows_b, :]
        a_f[rows_f, :] = hf
        a_b[rows_b, :] = hb
        return hf, hb

    zero = jnp.zeros((nb, LANES), F32)
    lax.fori_loop(0, seq, step, (zero, zero))

    def write(b, carry):
        rows = pl.ds(pl.multiple_of(b * seq, seq), seq)
        prow = pl.ds(pl.multiple_of(b * pitch, SUBLANES), seq)
        ob_ref[rows, :] = (a_f[prow, :] + a_b[prow, :]) * szb_ref[rows, :]
        return carry

    lax.fori_loop(0, nb, write, 0)
    for b in range(nb):
        hfl_ref[b] = a_f[b * pitch + seq - 1:b * pitch + seq, :]
        hbl_ref[b] = a_b[b * pitch:b * pitch + 1, :]


def _lru_lat_kernel(xb_ref, szb_ref, cw_ref, cb_ref, wcat_ref, bcat_ref, lam_ref, h0f_ref, h0b_ref,
                    ob_ref, hfl_ref, hbl_ref, xext, a_f, b_f, a_b, b_b, hin_f, hin_b):
    gw = GRID_W
    n = xb_ref.shape[0]
    nr = n // gw
    wi = lax.broadcasted_iota(jnp.int32, (gw, 1), 0)
    xext[2 * gw:2 * gw + n, :] = xb_ref[...]
    xext[0:gw, :] = jnp.where(wi >= 1, pltpu.roll(xb_ref[(nr - 2) * gw:(nr - 1) * gw, :], 1, 0), 0.0)
    xext[gw:2 * gw, :] = jnp.where(wi >= 1, pltpu.roll(xb_ref[(nr - 1) * gw:nr * gw, :], 1, 0), 0.0)
    xext[2 * gw + n:3 * gw + n, :] = jnp.where(wi <= gw - 2, pltpu.roll(xb_ref[0:gw, :], gw - 1, 0), 0.0)

    rows_per = 256

    def gates(q, carry):
        r0 = pl.multiple_of(q * rows_per, rows_per)
        xc = cb_ref[...] + cw_ref[0:1, :] * xext[pl.ds(r0, rows_per), :]
        for i in range(1, 4):
            xc = xc + cw_ref[i:i + 1, :] * xext[pl.ds(pl.multiple_of(r0 + i * gw, gw), rows_per), :]
        (af, bf), (ab, bb) = _lru_gates(xc, wcat_ref, bcat_ref, lam_ref)
        a_f[pl.ds(r0, rows_per), :] = af
        b_f[pl.ds(r0, rows_per), :] = bf
        a_b[pl.ds(r0, rows_per), :] = ab
        b_b[pl.ds(r0, rows_per), :] = bb
        return carry

    lax.fori_loop(0, n // rows_per, gates, 0)

    ng = gw // SUBLANES

    def col_scan(r, carry):
        acf, bcf, acb, bcb = carry
        rb = nr - 1 - r
        nacf, nbcf, nacb, nbcb = [], [], [], []
        for g in range(ng):
            rows_f = pl.ds(pl.multiple_of(r * gw + g * SUBLANES, SUBLANES), SUBLANES)
            rows_b = pl.ds(pl.multiple_of(rb * gw + g * SUBLANES, SUBLANES), SUBLANES)
            af = a_f[rows_f, :]
            ab = a_b[rows_b, :]
            caf = af * acf[g]
            cbf = af * bcf[g] + b_f[rows_f, :]
            cab = ab * acb[g]
            cbb = ab * bcb[g] + b_b[rows_b, :]
            a_f[rows_f, :] = caf
            b_f[rows_f, :] = cbf
            a_b[rows_b, :] = cab
            b_b[rows_b, :] = cbb
            nacf.append(caf)
            nbcf.append(cbf)
            nacb.append(cab)
            nbcb.append(cbb)
        return tuple(nacf), tuple(nbcf), tuple(nacb), tuple(nbcb)

    one = tuple(jnp.ones((SUBLANES, LANES), F32) for _ in range(ng))
    zero = tuple(jnp.zeros((SUBLANES, LANES), F32) for _ in range(ng))
    lax.fori_loop(0, nr, col_scan, (one, zero, one, zero))

    def carry_cols(w, carry):
        hf, hb = carry
        wb = gw - 1 - w
        hin_f[pl.ds(w, 1), :] = hf
        hin_b[pl.ds(wb, 1), :] = hb
        hf = a_f[pl.ds((nr - 1) * gw + w, 1), :] * hf + b_f[pl.ds((nr - 1) * gw + w, 1), :]
        hb = a_b[pl.ds(wb, 1), :] * hb + b_b[pl.ds(wb, 1), :]
        return hf, hb

    hf, hb = lax.fori_loop(0, gw, carry_cols, (h0f_ref[0], h0b_ref[0]))
    hfl_ref[0] = hf
    hbl_ref[0] = hb

    def finish(r, carry):
        rows = pl.ds(pl.multiple_of(r * gw, gw), gw)
        hsum = (a_f[rows, :] * hin_f[...] + b_f[rows, :]) + (a_b[rows, :] * hin_b[...] + b_b[rows, :])
        ob_ref[rows, :] = hsum * szb_ref[rows, :]
        return carry

    lax.fori_loop(0, nr, finish, 0)


def _lru_common_specs(n, blk):
    col = lambda *ids: ids[-1]
    return [
        pl.BlockSpec((n, LANES), blk),
        pl.BlockSpec((n, LANES), blk),
        pl.BlockSpec((4, LANES), lambda *ids: (0, col(*ids))),
        pl.BlockSpec((1, LANES), lambda *ids: (0, col(*ids))),
        pl.BlockSpec((1, LANES, 4 * LANES), lambda *ids: (col(*ids), 0, 0)),
        pl.BlockSpec((1, 1, 4 * LANES), lambda *ids: (col(*ids), 0, 0)),
        pl.BlockSpec((1, 1, 2 * LANES), lambda *ids: (col(*ids), 0, 0)),
    ]


def _lru_ctx_call(xb, szb, cw, cb, wcat, bcat, lam, batch):
    n = xb.shape[0]
    seq = n // batch
    nblk = D_MODEL // LANES
    st_spec = pl.BlockSpec((batch, 1, LANES), lambda j: (0, 0, j))
    return pl.pallas_call(
        functools.partial(_lru_ctx_kernel, seq),
        grid=(nblk,),
        in_specs=_lru_common_specs(n, lambda j: (0, j)),
        out_specs=[pl.BlockSpec((n, LANES), lambda j: (0, j)), st_spec, st_spec],
        out_shape=[
            jax.ShapeDtypeStruct((n, D_MODEL), F32),
            jax.ShapeDtypeStruct((batch, 1, D_MODEL), F32),
            jax.ShapeDtypeStruct((batch, 1, D_MODEL), F32),
        ],
        scratch_shapes=[pltpu.VMEM((batch * _ctx_pitch(seq), LANES), F32) for _ in range(4)],
        compiler_params=_params(("arbitrary",)),
        name="lru_ctx",
    )(xb, szb, cw, cb, wcat, bcat, lam)


def _lru_lat_call(xb, szb, cw, cb, wcat, bcat, lam, h0f, h0b, batch):
    n = xb.shape[0] // batch
    nblk = D_MODEL // LANES
    st_spec = pl.BlockSpec((1, 1, LANES), lambda b, j: (b, 0, j))
    return pl.pallas_call(
        _lru_lat_kernel,
        grid=(batch, nblk),
        in_specs=_lru_common_specs(n, lambda b, j: (b, j)) + [st_spec, st_spec],
        out_specs=[pl.BlockSpec((n, LANES), lambda b, j: (b, j)), st_spec, st_spec],
        out_shape=[
            jax.ShapeDtypeStruct((batch * n, D_MODEL), F32),
            jax.ShapeDtypeStruct((batch, 1, D_MODEL), F32),
            jax.ShapeDtypeStruct((batch, 1, D_MODEL), F32),
        ],
        scratch_shapes=[pltpu.VMEM((n + 3 * GRID_W, LANES), F32)]
        + [pltpu.VMEM((n, LANES), F32) for _ in range(4)]
        + [pltpu.VMEM((GRID_W, LANES), F32) for _ in range(2)],
        compiler_params=_params(("arbitrary", "arbitrary")),
        name="lru_lat",
    )(xb, szb, cw, cb, wcat, bcat, lam, h0f, h0b)


def _stage_c_kernel(tiles_per_seq, row0, x_ref, mod_ref, nw_ref, of_ref, ob_ref, sza_ref, olru_ref,
                    wg_ref, bg_ref, wpa_ref, wpb_ref, wo_ref, onw_ref, fnw_ref, y_ref):
    row = row0 + pl.program_id(0) // tiles_per_seq
    x = x_ref[...]
    h, gate = _modulated_norm(x, mod_ref, nw_ref, row)
    g = _sigmoid(_bdot(h, wg_ref[...]) + bg_ref[...])
    o = of_ref[...] + ob_ref[...]
    heads = []
    for hh in range(N_HEADS):
        oh = o[:, hh * D_HEAD:(hh + 1) * D_HEAD]
        heads.append(oh * lax.rsqrt(jnp.mean(oh * oh, axis=-1, keepdims=True) + EPS) * onw_ref[...])
    o_a = jnp.concatenate(heads, axis=1) * sza_ref[...]
    pa = _bdot(o_a, wpa_ref[...])
    pb = _bdot(olru_ref[...], wpb_ref[...])
    mixed = _bdot(g[:, 0:D_MODEL] * pa + g[:, D_MODEL:2 * D_MODEL] * pb, wo_ref[...])
    r = x + gate * mixed
    y_ref[...] = r * lax.rsqrt(jnp.mean(r * r, axis=-1, keepdims=True) + EPS) * fnw_ref[...]


def _stage_c_call(x, mod, norm_w, o_f, o_b, sza, olru, wg, bg, wpa, wpb, wo, onw, fnw,
                  tiles_per_seq, row0):
    n = x.shape[0]
    tm = TM_C
    tok = pl.BlockSpec((tm, D_MODEL), lambda i: (i, 0))
    return pl.pallas_call(
        functools.partial(_stage_c_kernel, tiles_per_seq, row0),
        grid=(n // tm,),
        in_specs=[
            tok,
            _const_spec((SUBLANES, 3 * D_MODEL)),
            _const_spec((1, D_MODEL)),
            tok, tok, tok, tok,
            _const_spec((D_MODEL, 2 * D_MODEL)),
            _const_spec((1, 2 * D_MODEL)),
            _const_spec((D_MODEL, D_MODEL)),
            _const_spec((D_MODEL, D_MODEL)),
            _const_spec((D_MODEL, D_MODEL)),
            _const_spec((1, D_HEAD)),
            _const_spec((1, D_MODEL)),
        ],
        out_specs=tok,
        out_shape=jax.ShapeDtypeStruct((n, D_MODEL), F32),
        compiler_params=_params(("arbitrary",)),
        name="stage_c",
    )(x, mod, norm_w, o_f, o_b, sza, olru, wg, bg, wpa, wpb, wo, onw, fnw)


def kernel(x_prompt, x_sample, state_a_fwd, state_a_bwd, state_b_fwd, state_b_bwd, c, c_ctx,
           norm_w, w_mod, b_mod, w_in, conv_a_w, conv_a_b, a_log_fwd, dt_bias_fwd, a_log_bwd,
           dt_bias_bwd, onorm_a_w, conv_b_w, conv_b_b, lru_wa_fwd, lru_ba_fwd, lru_wx_fwd,
           lru_bx_fwd, lru_lambda_fwd, lru_wa_bwd, lru_ba_bwd, lru_wx_bwd, lru_bx_bwd,
           lru_lambda_bwd, w_proj_a, w_proj_b, w_gate, b_gate, w_out, final_norm_w):
    bp, seq, d = x_prompt.shape
    bd, dseq, _ = x_sample.shape
    l = 0
    d3 = 3 * D_MODEL

    w_in_l = w_in[l]
    wqkv = w_in_l[:, 0:d3].astype(BF16)
    wza = w_in_l[:, d3:d3 + D_MODEL].astype(BF16)
    wxb = w_in_l[:, d3 + D_MODEL:d3 + 2 * D_MODEL].astype(BF16)
    wzb = w_in_l[:, d3 + 2 * D_MODEL:d3 + 3 * D_MODEL].astype(BF16)
    n_gate_cols = 4 * N_HEADS
    wgc = jnp.pad(w_in_l[:, d3 + 3 * D_MODEL:], ((0, 0), (0, LANES - n_gate_cols))).astype(BF16)
    zeros8 = jnp.zeros((N_HEADS,), F32)
    pad_lanes = jnp.zeros((LANES - n_gate_cols,), F32)
    alog_vec = jnp.concatenate([a_log_fwd[l], zeros8, a_log_bwd[l], zeros8, pad_lanes])[None, :]
    dtb_vec = jnp.concatenate([dt_bias_fwd[l], zeros8, dt_bias_bwd[l], zeros8, pad_lanes])[None, :]
    nw = norm_w[l][None, :]
    cvec = jnp.concatenate([c_ctx[None, :], c, jnp.zeros((SUBLANES - 1 - bd, d), F32)], axis=0)
    conv_aw = conv_a_w[l]
    conv_ab = conv_a_b[l][None, :]
    conv_bw = conv_b_w[l]
    conv_bb = conv_b_b[l][None, :]
    wcat = (0.5 * jnp.concatenate([lru_wa_fwd[l], lru_wx_fwd[l], lru_wa_bwd[l], lru_wx_bwd[l]],
                                  axis=2)).astype(BF16)
    nblk = D_MODEL // LANES
    blk = lambda v: v.reshape(nblk, 1, LANES)
    bcat = 0.5 * jnp.concatenate([blk(lru_ba_fwd[l]), blk(lru_bx_fwd[l]), blk(lru_ba_bwd[l]),
                                  blk(lru_bx_bwd[l])], axis=2)
    lam = jnp.concatenate([blk(lru_lambda_fwd[l]), blk(lru_lambda_bwd[l])], axis=2)
    wg = w_gate[l].astype(BF16)
    bg = b_gate[l][None, :]
    wpa = w_proj_a[l].astype(BF16)
    wpb = w_proj_b[l].astype(BF16)
    wo = w_out[l].astype(BF16)
    onw = onorm_a_w[l][None, :]
    fnw = final_norm_w[None, :]

    mod = _mod_call(cvec, w_mod[l], b_mod[l][None, :])

    def path(x2d, batch, mod_row0, tiles_per_seq_a, tiles_per_seq_c, s_af, s_ab, lru_fn):
        qkv, sza, xb, szb, gcols = _stage_a_call(
            x2d, mod, nw, wqkv, wza, wxb, wzb, wgc, alog_vec, dtb_vec, tiles_per_seq_a, mod_row0)
        o_f, saf, qn, kn, vn = _gdn_fwd_call(qkv, gcols, conv_aw, conv_ab, s_af, batch)
        o_b, sab = _gdn_bwd_call(qn, kn, vn, gcols, s_ab, batch)
        olru, sbf, sbb = lru_fn(xb, szb)
        y = _stage_c_call(x2d, mod, nw, o_f, o_b, sza, olru, wg, bg, wpa, wpb, wo, onw, fnw,
                          tiles_per_seq_c, mod_row0)
        return y, saf, sab, sbf, sbb

    n_ctx = bp * seq
    zero_state = jnp.zeros((bp, N_HEADS, D_HEAD, D_HEAD), F32)
    yp, saf, sab, sbf, sbb = path(
        x_prompt.reshape(n_ctx, d), bp, 0, n_ctx // TM_A, n_ctx // TM_C, zero_state, zero_state,
        lambda xb, szb: _lru_ctx_call(xb, szb, conv_bw, conv_bb, wcat, bcat, lam, bp))

    ys, _, _, _, _ = path(
        x_sample.reshape(bd * dseq, d), bd, 1, dseq // TM_A, dseq // TM_C,
        state_a_fwd[:, l], state_a_bwd[:, l],
        lambda xb, szb: _lru_lat_call(xb, szb, conv_bw, conv_bb, wcat, bcat, lam,
                                      state_b_fwd[:, l][:, None, :], state_b_bwd[:, l][:, None, :], bd))

    return (yp.reshape(bp, seq, d), ys.reshape(bd, dseq, d),
            saf[:, None], sab[:, None], sbf, sbb)
```

```python
import functools

import jax
import jax.numpy as jnp
from jax import lax
from jax.experimental import pallas as pl
from jax.experimental.pallas import tpu as pltpu

F32 = jnp.float32
BF16 = jnp.bfloat16
HIGHEST = lax.Precision.HIGHEST

D_MODEL = 1024
N_HEADS = 8
D_HEAD = 128
CHUNK = 64
GRID_W = 64
LRU_C = 8.0
EPS = 1e-6
LANES = 128
SUBLANES = 8
VMEM_LIMIT_BYTES = 56 * 1024 * 1024

TM_A = 256
TM_C = 256
TS_GDN = 256


def _bdot(a, b):
    return jnp.dot(a.astype(BF16), b.astype(BF16), preferred_element_type=F32)


def _sigmoid(x):
    return 1.0 / (1.0 + jnp.exp(-x))


def _softplus(x):
    return jnp.maximum(x, 0.0) + jnp.log1p(jnp.exp(-jnp.abs(x)))


def _sigmoid_t(x):
    return 0.5 * jnp.tanh(0.5 * x) + 0.5


def _const_spec(shape):
    n = len(shape)
    return pl.BlockSpec(shape, lambda *_: (0,) * n)


def _params(sem, vmem=VMEM_LIMIT_BYTES):
    return pltpu.CompilerParams(dimension_semantics=sem, vmem_limit_bytes=vmem)


def _mod_kernel(c_ref, w_ref, b_ref, o_ref):
    c = c_ref[...]
    o_ref[...] = _bdot(c * _sigmoid(c), w_ref[...]) + b_ref[...]


def _mod_call(cvec, w_mod, b_mod):
    n_col = 3
    return pl.pallas_call(
        _mod_kernel,
        grid=(n_col,),
        in_specs=[
            _const_spec((SUBLANES, D_MODEL)),
            pl.BlockSpec((D_MODEL, D_MODEL), lambda j: (0, j)),
            pl.BlockSpec((1, D_MODEL), lambda j: (0, j)),
        ],
        out_specs=pl.BlockSpec((SUBLANES, D_MODEL), lambda j: (0, j)),
        out_shape=jax.ShapeDtypeStruct((SUBLANES, 3 * D_MODEL), F32),
        compiler_params=_params(("arbitrary",)),
        name="mod_vectors",
    )(cvec, w_mod, b_mod)


def _modulated_norm(x, mod_ref, nw_ref, row):
    mod = mod_ref[pl.ds(row, 1), :]
    shift = mod[:, 0:D_MODEL]
    scale = mod[:, D_MODEL:2 * D_MODEL]
    gate = mod[:, 2 * D_MODEL:3 * D_MODEL]
    y = x * lax.rsqrt(jnp.mean(x * x, axis=-1, keepdims=True) + EPS) * nw_ref[...]
    return y * (1.0 + scale) + shift, gate


def _stage_a_kernel(seq_tiles, per_seq_mod, row0, x_ref, xp_ref, xn_ref, mod_ref, nw_ref, wqkv_ref,
                    wza_ref, wxb_ref, wzb_ref, wgc_ref, alog_ref, dtb_ref, cw_ref, cb_ref,
                    q_ref, k_ref, v_ref, sza_ref, xb_ref, szb_ref, gc_ref, xpad):
    tm = x_ref.shape[0]
    i = pl.program_id(0)
    ti = i % seq_tiles
    row = row0 + (i // seq_tiles if per_seq_mod else 0)
    x_ext = jnp.concatenate([xp_ref[...], x_ref[...], xn_ref[...]], axis=0)
    h_ext, _ = _modulated_norm(x_ext, mod_ref, nw_ref, row)
    hb_ext = h_ext.astype(BF16)
    for j in range(3):
        sl = slice(j * D_MODEL, (j + 1) * D_MODEL)
        xpad[:, sl] = jnp.dot(hb_ext, wqkv_ref[:, sl], preferred_element_type=F32)
    xpad[0:SUBLANES, :] = jnp.where(ti == 0, 0.0, xpad[0:SUBLANES, :])
    xpad[SUBLANES + tm:2 * SUBLANES + tm, :] = jnp.where(
        ti == seq_tiles - 1, 0.0, xpad[SUBLANES + tm:2 * SUBLANES + tm, :])

    def conv_block(col):
        cs = slice(col, col + D_HEAD)
        acc = cb_ref[:, cs] + xpad[SUBLANES - 2:SUBLANES - 2 + tm, cs] * cw_ref[0:1, cs]
        for t in range(1, 4):
            acc = acc + xpad[SUBLANES - 2 + t:SUBLANES - 2 + t + tm, cs] * cw_ref[t:t + 1, cs]
        return acc * _sigmoid(acc)

    def l2n(v, scale):
        return v * (lax.rsqrt(jnp.sum(v * v, axis=-1, keepdims=True) + EPS) * scale)

    for h in range(N_HEADS):
        hs = slice(h * D_HEAD, (h + 1) * D_HEAD)
        q_ref[:, hs] = l2n(conv_block(h * D_HEAD), D_HEAD ** -0.5).astype(BF16)
        k_ref[:, hs] = l2n(conv_block(D_MODEL + h * D_HEAD), 1.0).astype(BF16)
        v_ref[:, hs] = conv_block(2 * D_MODEL + h * D_HEAD).astype(BF16)

    hb = hb_ext[SUBLANES:SUBLANES + tm]
    za = jnp.dot(hb, wza_ref[...], preferred_element_type=F32)
    sza_ref[...] = za * _sigmoid(za)
    xb_ref[...] = jnp.dot(hb, wxb_ref[...], preferred_element_type=F32)
    zb = jnp.dot(hb, wzb_ref[...], preferred_element_type=F32)
    szb_ref[...] = zb * _sigmoid(zb)
    gc = jnp.dot(hb, wgc_ref[...], preferred_element_type=F32)
    lane = lax.broadcasted_iota(jnp.int32, gc.shape, 1)
    is_decay = (lane < 8) | ((lane >= 16) & (lane < 24))
    log_decay = -jnp.exp(alog_ref[...]) * _softplus(gc + dtb_ref[...])
    gc_ref[...] = jnp.where(is_decay, log_decay, _sigmoid(gc))


def _stage_a_call(x, mod, norm_w, wqkv, wza, wxb, wzb, wgc, alog_vec, dtb_vec, conv_w, conv_b,
                  seq_tiles, per_seq_mod, row0):
    n = x.shape[0]
    tm = TM_A
    blocks8 = tm // SUBLANES
    tok = lambda w: pl.BlockSpec((tm, w), lambda i: (i, 0))
    return pl.pallas_call(
        functools.partial(_stage_a_kernel, seq_tiles, per_seq_mod, row0),
        grid=(n // tm,),
        in_specs=[
            tok(D_MODEL),
            pl.BlockSpec((SUBLANES, D_MODEL), lambda i: (jnp.maximum(i * blocks8 - 1, 0), 0)),
            pl.BlockSpec((SUBLANES, D_MODEL),
                         lambda i: (jnp.minimum((i + 1) * blocks8, n // SUBLANES - 1), 0)),
            _const_spec((SUBLANES, 3 * D_MODEL)),
            _const_spec((1, D_MODEL)),
            _const_spec((D_MODEL, 3 * D_MODEL)),
            _const_spec((D_MODEL, D_MODEL)),
            _const_spec((D_MODEL, D_MODEL)),
            _const_spec((D_MODEL, D_MODEL)),
            _const_spec((D_MODEL, LANES)),
            _const_spec((1, LANES)),
            _const_spec((1, LANES)),
            _const_spec((4, 3 * D_MODEL)),
            _const_spec((1, 3 * D_MODEL)),
        ],
        out_specs=[tok(D_MODEL), tok(D_MODEL), tok(D_MODEL), tok(D_MODEL), tok(D_MODEL),
                   tok(D_MODEL), tok(LANES)],
        out_shape=[
            jax.ShapeDtypeStruct((n, D_MODEL), BF16),
            jax.ShapeDtypeStruct((n, D_MODEL), BF16),
            jax.ShapeDtypeStruct((n, D_MODEL), BF16),
            jax.ShapeDtypeStruct((n, D_MODEL), F32),
            jax.ShapeDtypeStruct((n, D_MODEL), F32),
            jax.ShapeDtypeStruct((n, D_MODEL), F32),
            jax.ShapeDtypeStruct((n, LANES), F32),
        ],
        scratch_shapes=[pltpu.VMEM((tm + 2 * SUBLANES, 3 * D_MODEL), F32)],
        compiler_params=_params(("arbitrary",)),
        name="stage_a",
    )(x, x, x, mod, norm_w, wqkv, wza, wxb, wzb, wgc, alog_vec, dtb_vec, conv_w, conv_b)


def _chunk_cumsum(x, reverse):
    n = x.shape[0]
    row = lax.broadcasted_iota(jnp.int32, x.shape, 0)
    s = 1
    while s < n:
        if reverse:
            x = x + jnp.where(row < n - s, pltpu.roll(x, n - s, 0), 0.0)
        else:
            x = x + jnp.where(row >= s, pltpu.roll(x, s, 0), 0.0)
        s *= 2
    return x


def _delta_rule_tile(reverse, q_ref, k_ref, v_ref, gc_ref, state, o_ref,
                     u_scr, wq_scr, qk_scr, kd_scr, gt_scr):
    ts = q_ref.shape[0]
    nc = ts // CHUNK
    nh = N_HEADS
    ri = lax.broadcasted_iota(jnp.int32, (CHUNK, CHUNK), 0)
    ci = lax.broadcasted_iota(jnp.int32, (CHUNK, CHUNK), 1)
    if reverse:
        ri, ci = ci, ri
    incl = ri >= ci
    strict = ri > ci
    eye = ri == ci
    level_masks = []
    for lg in range(6):
        s = 1 << lg
        same = (ri >> (lg + 1)) == (ci >> (lg + 1))
        level_masks.append(same & ((ri & (2 * s - 1)) >= s) & ((ci & (2 * s - 1)) < s))
    g_off = 16 if reverse else 0
    b_off = 24 if reverse else 8

    probs = [(c, h) for c in range(nc) for h in range(nh)]
    rows = lambda c: slice(c * CHUNK, (c + 1) * CHUNK)
    cols = lambda h: slice(h * D_HEAD, (h + 1) * D_HEAD)
    lane = lambda off, h: slice(off + h, off + h + 1)

    gcol = [gc_ref[rows(c), :] for c in range(nc)]
    g_cum = [_chunk_cumsum(g, reverse) for g in gcol]
    g_cum_t = [g.T for g in g_cum]
    last = 0 if reverse else CHUNK - 1
    g_tot = [g[last:last + 1, :] for g in g_cum]
    for c in range(nc):
        gt_scr[c] = jnp.broadcast_to(g_tot[c], (SUBLANES, LANES))
    g_i = [g_cum[c][:, lane(g_off, h)] for c, h in probs]
    g_j = [g_cum_t[c][lane(g_off, h), :] for c, h in probs]
    beta = [gcol[c][:, lane(b_off, h)] for c, h in probs]
    k = [k_ref[rows(c), cols(h)].astype(F32) for c, h in probs]
    q = [q_ref[rows(c), cols(h)].astype(F32) for c, h in probs]
    kb = [a * b for a, b in zip(k, beta)]
    prod = [lax.dot_general(jnp.concatenate([a, b], axis=0).astype(BF16), c_.astype(BF16),
                            (((1,), (1,)), ((), ())), preferred_element_type=F32)
            for a, b, c_ in zip(kb, q, k)]
    decay = [jnp.exp(jnp.where(incl, a - b, -1e30)) for a, b in zip(g_i, g_j)]
    a_mat = [jnp.where(strict, p[0:CHUNK] * d, 0.0) for p, d in zip(prod, decay)]
    for i, (p, d) in enumerate(zip(prod, decay)):
        qk_scr[i] = (p[CHUNK:2 * CHUNK] * d).astype(BF16)
    t_inv = [jnp.where(eye, 1.0, 0.0) - jnp.where(level_masks[0], a, 0.0) for a in a_mat]
    for lvl in range(1, 6):
        x = [_bdot(jnp.where(level_masks[lvl], a, 0.0), t) for a, t in zip(a_mat, t_inv)]
        t_inv = [t - _bdot(t, x_) for t, x_ in zip(t_inv, x)]
    e_g = [jnp.exp(g) for g in g_i]
    v = [v_ref[rows(c), cols(h)].astype(F32) for c, h in probs]
    rhs = [jnp.concatenate([v_ * b, kb_ * e], axis=1)
           for v_, b, kb_, e in zip(v, beta, kb, e_g)]
    sol = [r + _bdot(jnp.where(eye, 0.0, t), r) for r, t in zip(rhs, t_inv)]
    for i, (c, h) in enumerate(probs):
        u_scr[i] = sol[i][:, 0:D_HEAD]
        wq_scr[i] = jnp.concatenate([sol[i][:, D_HEAD:2 * D_HEAD], q[i] * e_g[i]],
                                    axis=0).astype(BF16)
        kd_scr[i] = (k[i] * jnp.exp(g_tot[c][:, lane(g_off, h)] - g_i[i])).astype(BF16)

    for c in (range(nc - 1, -1, -1) if reverse else range(nc)):
        idx = [c * nh + h for h in range(nh)]
        s_old = [state[h] for h in range(nh)]
        ws_qs = [jnp.dot(wq_scr[i], s.astype(BF16), preferred_element_type=F32)
                 for i, s in zip(idx, s_old)]
        v_new = [(u_scr[i] - r[0:CHUNK]).astype(BF16) for i, r in zip(idx, ws_qs)]
        for h in range(nh):
            o_ref[rows(c), cols(h)] = ws_qs[h][CHUNK:2 * CHUNK] + jnp.dot(
                qk_scr[idx[h]], v_new[h], preferred_element_type=F32)
        for h in range(nh):
            e_tot = jnp.exp(gt_scr[c][0:1, lane(g_off, h)])
            state[h] = s_old[h] * e_tot + lax.dot_general(
                kd_scr[idx[h]], v_new[h], (((0,), (0,)), ((), ())), preferred_element_type=F32)


def _gdn_scratch(ts):
    nprob = (ts // CHUNK) * N_HEADS
    return [
        pltpu.VMEM((N_HEADS, D_HEAD, D_HEAD), F32),
        pltpu.VMEM((nprob, CHUNK, D_HEAD), F32),
        pltpu.VMEM((nprob, 2 * CHUNK, D_HEAD), BF16),
        pltpu.VMEM((nprob, CHUNK, CHUNK), BF16),
        pltpu.VMEM((nprob, CHUNK, D_HEAD), BF16),
        pltpu.VMEM((ts // CHUNK, SUBLANES, LANES), F32),
    ]


def _gdn_kernel(reverse, nt, q_ref, k_ref, v_ref, gc_ref, s0_ref, o_ref, sfin_ref, state, *scr):
    t = pl.program_id(1)

    @pl.when(t == 0)
    def _init():
        state[...] = s0_ref[0]

    _delta_rule_tile(reverse, q_ref, k_ref, v_ref, gc_ref, state, o_ref, *scr)

    @pl.when(t == nt - 1)
    def _fin():
        sfin_ref[0] = state[...]


def _gdn_call(q, k, v, gcols, s0, batch, reverse):
    n = q.shape[0]
    ts = TS_GDN
    nt = n // batch // ts
    tile = lambda b, t: b * nt + ((nt - 1 - t) if reverse else t)
    state_spec = pl.BlockSpec((1, N_HEADS, D_HEAD, D_HEAD), lambda b, t: (b, 0, 0, 0))
    tok = pl.BlockSpec((ts, D_MODEL), lambda b, t: (tile(b, t), 0))
    return pl.pallas_call(
        functools.partial(_gdn_kernel, reverse, nt),
        grid=(batch, nt),
        in_specs=[tok, tok, tok, pl.BlockSpec((ts, LANES), lambda b, t: (tile(b, t), 0)), state_spec],
        out_specs=[tok, state_spec],
        out_shape=[
            jax.ShapeDtypeStruct((n, D_MODEL), F32),
            jax.ShapeDtypeStruct((batch, N_HEADS, D_HEAD, D_HEAD), F32),
        ],
        scratch_shapes=_gdn_scratch(ts),
        compiler_params=_params(("arbitrary", "arbitrary")),
        name="gdn_bwd" if reverse else "gdn_fwd",
    )(q, k, v, gcols, s0)


def _lru_gates(xc, wcat_ref, bcat_ref, lam_ref):
    th = jnp.tanh(_bdot(xc, wcat_ref[0]) + bcat_ref[0])
    half_c_sp = (-0.5 * LRU_C) * _softplus(-lam_ref[0])
    half_x = 0.5 * xc
    out = []
    for d in range(2):
        th_r = th[:, (2 * d) * LANES:(2 * d + 1) * LANES]
        th_i = th[:, (2 * d + 1) * LANES:(2 * d + 2) * LANES]
        hcs = half_c_sp[:, d * LANES:(d + 1) * LANES]
        log_a = th_r * hcs + hcs
        a = jnp.exp(log_a)
        one_m_a2 = jnp.tanh(log_a) * (-1.0 - a * a)
        out.append((a, jnp.sqrt(one_m_a2) * ((th_i + 1.0) * half_x)))
    return out


def _ctx_pitch(seq):
    tiles = seq // SUBLANES
    return (tiles + 1 - tiles % 2) * SUBLANES


def _lru_ctx_kernel(seq, xb_ref, szb_ref, cw_ref, cb_ref, wcat_ref, bcat_ref, lam_ref,
                    ob_ref, hfl_ref, hbl_ref, a_f, b_f, a_b, b_b):
    nb = xb_ref.shape[0] // seq
    pitch = _ctx_pitch(seq)
    rowi = lax.broadcasted_iota(jnp.int32, (seq, 1), 0)

    def per_seq(b, carry):
        x = xb_ref[pl.ds(pl.multiple_of(b * seq, seq), seq), :]
        r0 = pl.multiple_of(b * pitch, SUBLANES)
        xm2 = jnp.where(rowi >= 2, pltpu.roll(x, 2, 0), 0.0)
        xm1 = jnp.where(rowi >= 1, pltpu.roll(x, 1, 0), 0.0)
        xp1 = jnp.where(rowi <= seq - 2, pltpu.roll(x, seq - 1, 0), 0.0)
        xc = (cw_ref[0:1, :] * xm2 + cw_ref[1:2, :] * xm1 + cw_ref[2:3, :] * x
              + cw_ref[3:4, :] * xp1 + cb_ref[...])
        (af, bf), (ab, bb) = _lru_gates(xc, wcat_ref, bcat_ref, lam_ref)
        a_f[pl.ds(r0, seq), :] = af
        b_f[pl.ds(r0, seq), :] = bf
        a_b[pl.ds(r0, seq), :] = ab
        b_b[pl.ds(r0, seq), :] = bb
        return carry

    lax.fori_loop(0, nb, per_seq, 0)

    def step(t, carry):
        hf, hb = carry
        rows_f = pl.ds(t, nb, stride=pitch)
        rows_b = pl.ds(seq - 1 - t, nb, stride=pitch)
        hf = a_f[rows_f, :] * hf + b_f[rows_f, :]
        hb = a_b[rows_b, :] * hb + b_b[rows_b, :]
        a_f[rows_f, :] = hf
        a_b[rows_b, :] = hb
        return hf, hb

    zero = jnp.zeros((nb, LANES), F32)
    lax.fori_loop(0, seq, step, (zero, zero))

    def write(b, carry):
        rows = pl.ds(pl.multiple_of(b * seq, seq), seq)
        prow = pl.ds(pl.multiple_of(b * pitch, SUBLANES), seq)
        ob_ref[rows, :] = (a_f[prow, :] + a_b[prow, :]) * szb_ref[rows, :]
        return carry

    lax.fori_loop(0, nb, write, 0)
    for b in range(nb):
        hfl_ref[b] = a_f[b * pitch + seq - 1:b * pitch + seq, :]
        hbl_ref[b] = a_b[b * pitch:b * pitch + 1, :]


def _lru_lat_kernel(xb_ref, szb_ref, cw_ref, cb_ref, wcat_ref, bcat_ref, lam_ref, h0f_ref, h0b_ref,
                    ob_ref, hfl_ref, hbl_ref, xext, a_f, b_f, a_b, b_b, hin_f, hin_b):
    gw = GRID_W
    n = xb_ref.shape[0]
    nr = n // gw
    wi = lax.broadcasted_iota(jnp.int32, (gw, 1), 0)
    xext[2 * gw:2 * gw + n, :] = xb_ref[...]
    xext[0:gw, :] = jnp.where(wi >= 1, pltpu.roll(xb_ref[(nr - 2) * gw:(nr - 1) * gw, :], 1, 0), 0.0)
    xext[gw:2 * gw, :] = jnp.where(wi >= 1, pltpu.roll(xb_ref[(nr - 1) * gw:nr * gw, :], 1, 0), 0.0)
    xext[2 * gw + n:3 * gw + n, :] = jnp.where(wi <= gw - 2, pltpu.roll(xb_ref[0:gw, :], gw - 1, 0), 0.0)

    rows_per = 256

    def gates(q, carry):
        r0 = pl.multiple_of(q * rows_per, rows_per)
        xc = cb_ref[...] + cw_ref[0:1, :] * xext[pl.ds(r0, rows_per), :]
        for i in range(1, 4):
            xc = xc + cw_ref[i:i + 1, :] * xext[pl.ds(pl.multiple_of(r0 + i * gw, gw), rows_per), :]
        (af, bf), (ab, bb) = _lru_gates(xc, wcat_ref, bcat_ref, lam_ref)
        a_f[pl.ds(r0, rows_per), :] = af
        b_f[pl.ds(r0, rows_per), :] = bf
        a_b[pl.ds(r0, rows_per), :] = ab
        b_b[pl.ds(r0, rows_per), :] = bb
        return carry

    lax.fori_loop(0, n // rows_per, gates, 0)

    ng = gw // SUBLANES

    def col_scan(r, carry):
        acf, bcf, acb, bcb = carry
        rb = nr - 1 - r
        nacf, nbcf, nacb, nbcb = [], [], [], []
        for g in range(ng):
            rows_f = pl.ds(pl.multiple_of(r * gw + g * SUBLANES, SUBLANES), SUBLANES)
            rows_b = pl.ds(pl.multiple_of(rb * gw + g * SUBLANES, SUBLANES), SUBLANES)
            af = a_f[rows_f, :]
            ab = a_b[rows_b, :]
            caf = af * acf[g]
            cbf = af * bcf[g] + b_f[rows_f, :]
            cab = ab * acb[g]
            cbb = ab * bcb[g] + b_b[rows_b, :]
            a_f[rows_f, :] = caf
            b_f[rows_f, :] = cbf
            a_b[rows_b, :] = cab
            b_b[rows_b, :] = cbb
            nacf.append(caf)
            nbcf.append(cbf)
            nacb.append(cab)
            nbcb.append(cbb)
        return tuple(nacf), tuple(nbcf), tuple(nacb), tuple(nbcb)

    one = tuple(jnp.ones((SUBLANES, LANES), F32) for _ in range(ng))
    zero = tuple(jnp.zeros((SUBLANES, LANES), F32) for _ in range(ng))
    lax.fori_loop(0, nr, col_scan, (one, zero, one, zero))

    def carry_cols(w, carry):
        hf, hb = carry
        wb = gw - 1 - w
        hin_f[pl.ds(w, 1), :] = hf
        hin_b[pl.ds(wb, 1), :] = hb
        hf = a_f[pl.ds((nr - 1) * gw + w, 1), :] * hf + b_f[pl.ds((nr - 1) * gw + w, 1), :]
        hb = a_b[pl.ds(wb, 1), :] * hb + b_b[pl.ds(wb, 1), :]
        return hf, hb

    hf, hb = lax.fori_loop(0, gw, carry_cols, (h0f_ref[0], h0b_ref[0]))
    hfl_ref[0] = hf
    hbl_ref[0] = hb

    def finish(r, carry):
        rows = pl.ds(pl.multiple_of(r * gw, gw), gw)
        hsum = (a_f[rows, :] * hin_f[...] + b_f[rows, :]) + (a_b[rows, :] * hin_b[...] + b_b[rows, :])
        ob_ref[rows, :] = hsum * szb_ref[rows, :]
        return carry

    lax.fori_loop(0, nr, finish, 0)


def _lru_common_specs(n, blk):
    col = lambda *ids: ids[-1]
    return [
        pl.BlockSpec((n, LANES), blk),
        pl.BlockSpec((n, LANES), blk),
        pl.BlockSpec((4, LANES), lambda *ids: (0, col(*ids))),
        pl.BlockSpec((1, LANES), lambda *ids: (0, col(*ids))),
        pl.BlockSpec((1, LANES, 4 * LANES), lambda *ids: (col(*ids), 0, 0)),
        pl.BlockSpec((1, 1, 4 * LANES), lambda *ids: (col(*ids), 0, 0)),
        pl.BlockSpec((1, 1, 2 * LANES), lambda *ids: (col(*ids), 0, 0)),
    ]


def _lru_ctx_call(xb, szb, cw, cb, wcat, bcat, lam, batch):
    n = xb.shape[0]
    seq = n // batch
    nblk = D_MODEL // LANES
    st_spec = pl.BlockSpec((batch, 1, LANES), lambda j: (0, 0, j))
    return pl.pallas_call(
        functools.partial(_lru_ctx_kernel, seq),
        grid=(nblk,),
        in_specs=_lru_common_specs(n, lambda j: (0, j)),
        out_specs=[pl.BlockSpec((n, LANES), lambda j: (0, j)), st_spec, st_spec],
        out_shape=[
            jax.ShapeDtypeStruct((n, D_MODEL), F32),
            jax.ShapeDtypeStruct((batch, 1, D_MODEL), F32),
            jax.ShapeDtypeStruct((batch, 1, D_MODEL), F32),
        ],
        scratch_shapes=[pltpu.VMEM((batch * _ctx_pitch(seq), LANES), F32) for _ in range(4)],
        compiler_params=_params(("arbitrary",)),
        name="lru_ctx",
    )(xb, szb, cw, cb, wcat, bcat, lam)


def _lru_lat_call(xb, szb, cw, cb, wcat, bcat, lam, h0f, h0b, batch):
    n = xb.shape[0] // batch
    nblk = D_MODEL // LANES
    st_spec = pl.BlockSpec((1, 1, LANES), lambda b, j: (b, 0, j))
    return pl.pallas_call(
        _lru_lat_kernel,
        grid=(batch, nblk),
        in_specs=_lru_common_specs(n, lambda b, j: (b, j)) + [st_spec, st_spec],
        out_specs=[pl.BlockSpec((n, LANES), lambda b, j: (b, j)), st_spec, st_spec],
        out_shape=[
            jax.ShapeDtypeStruct((batch * n, D_MODEL), F32),
            jax.ShapeDtypeStruct((batch, 1, D_MODEL), F32),
            jax.ShapeDtypeStruct((batch, 1, D_MODEL), F32),
        ],
        scratch_shapes=[pltpu.VMEM((n + 3 * GRID_W, LANES), F32)]
        + [pltpu.VMEM((n, LANES), F32) for _ in range(4)]
        + [pltpu.VMEM((GRID_W, LANES), F32) for _ in range(2)],
        compiler_params=_params(("arbitrary", "arbitrary")),
        name="lru_lat",
    )(xb, szb, cw, cb, wcat, bcat, lam, h0f, h0b)


def _stage_c_kernel(tiles_per_seq, row0, x_ref, mod_ref, nw_ref, of_ref, ob_ref, sza_ref, olru_ref,
                    wg_ref, bg_ref, wpa_ref, wpb_ref, wo_ref, onw_ref, fnw_ref, y_ref):
    row = row0 + pl.program_id(0) // tiles_per_seq
    x = x_ref[...]
    h, gate = _modulated_norm(x, mod_ref, nw_ref, row)
    g = _sigmoid(_bdot(h, wg_ref[...]) + bg_ref[...])
    o = of_ref[...] + ob_ref[...]
    heads = []
    for hh in range(N_HEADS):
        oh = o[:, hh * D_HEAD:(hh + 1) * D_HEAD]
        heads.append(oh * lax.rsqrt(jnp.mean(oh * oh, axis=-1, keepdims=True) + EPS) * onw_ref[...])
    o_a = jnp.concatenate(heads, axis=1) * sza_ref[...]
    pa = _bdot(o_a, wpa_ref[...])
    pb = _bdot(olru_ref[...], wpb_ref[...])
    mixed = _bdot(g[:, 0:D_MODEL] * pa + g[:, D_MODEL:2 * D_MODEL] * pb, wo_ref[...])
    r = x + gate * mixed
    y_ref[...] = r * lax.rsqrt(jnp.mean(r * r, axis=-1, keepdims=True) + EPS) * fnw_ref[...]


def _stage_c_call(x, mod, norm_w, o_f, o_b, sza, olru, wg, bg, wpa, wpb, wo, onw, fnw,
                  tiles_per_seq, row0):
    n = x.shape[0]
    tm = TM_C
    tok = pl.BlockSpec((tm, D_MODEL), lambda i: (i, 0))
    return pl.pallas_call(
        functools.partial(_stage_c_kernel, tiles_per_seq, row0),
        grid=(n // tm,),
        in_specs=[
            tok,
            _const_spec((SUBLANES, 3 * D_MODEL)),
            _const_spec((1, D_MODEL)),
            tok, tok, tok, tok,
            _const_spec((D_MODEL, 2 * D_MODEL)),
            _const_spec((1, 2 * D_MODEL)),
            _const_spec((D_MODEL, D_MODEL)),
            _const_spec((D_MODEL, D_MODEL)),
            _const_spec((D_MODEL, D_MODEL)),
            _const_spec((1, D_HEAD)),
            _const_spec((1, D_MODEL)),
        ],
        out_specs=tok,
        out_shape=jax.ShapeDtypeStruct((n, D_MODEL), F32),
        compiler_params=_params(("arbitrary",)),
        name="stage_c",
    )(x, mod, norm_w, o_f, o_b, sza, olru, wg, bg, wpa, wpb, wo, onw, fnw)


def kernel(x_prompt, x_sample, state_a_fwd, state_a_bwd, state_b_fwd, state_b_bwd, c, c_ctx,
           norm_w, w_mod, b_mod, w_in, conv_a_w, conv_a_b, a_log_fwd, dt_bias_fwd, a_log_bwd,
           dt_bias_bwd, onorm_a_w, conv_b_w, conv_b_b, lru_wa_fwd, lru_ba_fwd, lru_wx_fwd,
           lru_bx_fwd, lru_lambda_fwd, lru_wa_bwd, lru_ba_bwd, lru_wx_bwd, lru_bx_bwd,
           lru_lambda_bwd, w_proj_a, w_proj_b, w_gate, b_gate, w_out, final_norm_w):
    bp, seq, d = x_prompt.shape
    bd, dseq, _ = x_sample.shape
    l = 0
    d3 = 3 * D_MODEL

    w_in_l = w_in[l]
    wqkv = w_in_l[:, 0:d3].astype(BF16)
    wza = w_in_l[:, d3:d3 + D_MODEL].astype(BF16)
    wxb = w_in_l[:, d3 + D_MODEL:d3 + 2 * D_MODEL].astype(BF16)
    wzb = w_in_l[:, d3 + 2 * D_MODEL:d3 + 3 * D_MODEL].astype(BF16)
    n_gate_cols = 4 * N_HEADS
    wgc = jnp.pad(w_in_l[:, d3 + 3 * D_MODEL:], ((0, 0), (0, LANES - n_gate_cols))).astype(BF16)
    zeros8 = jnp.zeros((N_HEADS,), F32)
    pad_lanes = jnp.zeros((LANES - n_gate_cols,), F32)
    alog_vec = jnp.concatenate([a_log_fwd[l], zeros8, a_log_bwd[l], zeros8, pad_lanes])[None, :]
    dtb_vec = jnp.concatenate([dt_bias_fwd[l], zeros8, dt_bias_bwd[l], zeros8, pad_lanes])[None, :]
    nw = norm_w[l][None, :]
    cvec = jnp.concatenate([c_ctx[None, :], c, jnp.zeros((SUBLANES - 1 - bd, d), F32)], axis=0)
    conv_aw = conv_a_w[l]
    conv_ab = conv_a_b[l][None, :]
    conv_bw = conv_b_w[l]
    conv_bb = conv_b_b[l][None, :]
    wcat = (0.5 * jnp.concatenate([lru_wa_fwd[l], lru_wx_fwd[l], lru_wa_bwd[l], lru_wx_bwd[l]],
                                  axis=2)).astype(BF16)
    nblk = D_MODEL // LANES
    blk = lambda v: v.reshape(nblk, 1, LANES)
    bcat = 0.5 * jnp.concatenate([blk(lru_ba_fwd[l]), blk(lru_bx_fwd[l]), blk(lru_ba_bwd[l]),
                                  blk(lru_bx_bwd[l])], axis=2)
    lam = jnp.concatenate([blk(lru_lambda_fwd[l]), blk(lru_lambda_bwd[l])], axis=2)
    wg = w_gate[l].astype(BF16)
    bg = b_gate[l][None, :]
    wpa = w_proj_a[l].astype(BF16)
    wpb = w_proj_b[l].astype(BF16)
    wo = w_out[l].astype(BF16)
    onw = onorm_a_w[l][None, :]
    fnw = final_norm_w[None, :]

    mod = _mod_call(cvec, w_mod[l], b_mod[l][None, :])

    def path(x2d, batch, mod_row0, per_seq_mod, tiles_per_seq_c, s_af, s_ab, lru_fn):
        seq_tiles = x2d.shape[0] // batch // TM_A
        qn, kn, vn, sza, xb, szb, gcols = _stage_a_call(
            x2d, mod, nw, wqkv, wza, wxb, wzb, wgc, alog_vec, dtb_vec, conv_aw, conv_ab,
            seq_tiles, per_seq_mod, mod_row0)
        o_f, saf = _gdn_call(qn, kn, vn, gcols, s_af, batch, reverse=False)
        o_b, sab = _gdn_call(qn, kn, vn, gcols, s_ab, batch, reverse=True)
        olru, sbf, sbb = lru_fn(xb, szb)
        y = _stage_c_call(x2d, mod, nw, o_f, o_b, sza, olru, wg, bg, wpa, wpb, wo, onw, fnw,
                          tiles_per_seq_c, mod_row0)
        return y, saf, sab, sbf, sbb

    n_ctx = bp * seq
    zero_state = jnp.zeros((bp, N_HEADS, D_HEAD, D_HEAD), F32)
    yp, saf, sab, sbf, sbb = path(
        x_prompt.reshape(n_ctx, d), bp, 0, False, n_ctx // TM_C, zero_state, zero_state,
        lambda xb, szb: _lru_ctx_call(xb, szb, conv_bw, conv_bb, wcat, bcat, lam, bp))

    ys, _, _, _, _ = path(
        x_sample.reshape(bd * dseq, d), bd, 1, True, dseq // TM_C,
        state_a_fwd[:, l], state_a_bwd[:, l],
        lambda xb, szb: _lru_lat_call(xb, szb, conv_bw, conv_bb, wcat, bcat, lam,
                                      state_b_fwd[:, l][:, None, :], state_b_bwd[:, l][:, None, :], bd))

    return (yp.reshape(bp, seq, d), ys.reshape(bd, dseq, d),
            saf[:, None], sab[:, None], sbf, sbb)
```

```python
import functools

import jax
import jax.numpy as jnp
from jax import lax
from jax.experimental import pallas as pl
from jax.experimental.pallas import tpu as pltpu

F32 = jnp.float32
BF16 = jnp.bfloat16
HIGHEST = lax.Precision.HIGHEST

D_MODEL = 1024
N_HEADS = 8
D_HEAD = 128
CHUNK = 64
GRID_W = 64
LRU_C = 8.0
EPS = 1e-6
LANES = 128
SUBLANES = 8
VMEM_LIMIT_BYTES = 56 * 1024 * 1024

TM_A = 256
TM_C = 256
TS_GDN = 256


def _bdot(a, b):
    return jnp.dot(a.astype(BF16), b.astype(BF16), preferred_element_type=F32)


def _sigmoid(x):
    return 1.0 / (1.0 + jnp.exp(-x))


def _softplus(x):
    return jnp.maximum(x, 0.0) + jnp.log1p(jnp.exp(-jnp.abs(x)))


def _sigmoid_t(x):
    return 0.5 * jnp.tanh(0.5 * x) + 0.5


def _const_spec(shape):
    n = len(shape)
    return pl.BlockSpec(shape, lambda *_: (0,) * n)


def _params(sem, vmem=VMEM_LIMIT_BYTES):
    return pltpu.CompilerParams(dimension_semantics=sem, vmem_limit_bytes=vmem)


def _mod_kernel(c_ref, w_ref, b_ref, o_ref):
    c = c_ref[...]
    o_ref[...] = _bdot(c * _sigmoid(c), w_ref[...]) + b_ref[...]


def _mod_call(cvec, w_mod, b_mod):
    n_col = 3
    return pl.pallas_call(
        _mod_kernel,
        grid=(n_col,),
        in_specs=[
            _const_spec((SUBLANES, D_MODEL)),
            pl.BlockSpec((D_MODEL, D_MODEL), lambda j: (0, j)),
            pl.BlockSpec((1, D_MODEL), lambda j: (0, j)),
        ],
        out_specs=pl.BlockSpec((SUBLANES, D_MODEL), lambda j: (0, j)),
        out_shape=jax.ShapeDtypeStruct((SUBLANES, 3 * D_MODEL), F32),
        compiler_params=_params(("arbitrary",)),
        name="mod_vectors",
    )(cvec, w_mod, b_mod)


def _modulated_norm(x, mod_ref, nw_ref, row):
    mod = mod_ref[pl.ds(row, 1), :]
    shift = mod[:, 0:D_MODEL]
    scale = mod[:, D_MODEL:2 * D_MODEL]
    gate = mod[:, 2 * D_MODEL:3 * D_MODEL]
    y = x * lax.rsqrt(jnp.mean(x * x, axis=-1, keepdims=True) + EPS) * nw_ref[...]
    return y * (1.0 + scale) + shift, gate


def _stage_a_kernel(seq_tiles, per_seq_mod, row0, x_ref, xp_ref, xn_ref, mod_ref, nw_ref, wqkv_ref,
                    wza_ref, wxb_ref, wzb_ref, wgc_ref, alog_ref, dtb_ref, cw_ref, cb_ref,
                    q_ref, k_ref, v_ref, sza_ref, xb_ref, szb_ref, gc_ref, xpad):
    tm = x_ref.shape[0]
    i = pl.program_id(0)
    ti = i % seq_tiles
    row = row0 + (i // seq_tiles if per_seq_mod else 0)
    x_ext = jnp.concatenate([xp_ref[...], x_ref[...], xn_ref[...]], axis=0)
    h_ext, _ = _modulated_norm(x_ext, mod_ref, nw_ref, row)
    hb_ext = h_ext.astype(BF16)
    for j in range(3):
        sl = slice(j * D_MODEL, (j + 1) * D_MODEL)
        xpad[:, sl] = jnp.dot(hb_ext, wqkv_ref[:, sl], preferred_element_type=F32)
    xpad[0:SUBLANES, :] = jnp.where(ti == 0, 0.0, xpad[0:SUBLANES, :])
    xpad[SUBLANES + tm:2 * SUBLANES + tm, :] = jnp.where(
        ti == seq_tiles - 1, 0.0, xpad[SUBLANES + tm:2 * SUBLANES + tm, :])

    def conv_block(col):
        cs = slice(col, col + D_HEAD)
        acc = cb_ref[:, cs] + xpad[SUBLANES - 2:SUBLANES - 2 + tm, cs] * cw_ref[0:1, cs]
        for t in range(1, 4):
            acc = acc + xpad[SUBLANES - 2 + t:SUBLANES - 2 + t + tm, cs] * cw_ref[t:t + 1, cs]
        return acc * _sigmoid(acc)

    def l2n(v, scale):
        return v * (lax.rsqrt(jnp.sum(v * v, axis=-1, keepdims=True) + EPS) * scale)

    for h in range(N_HEADS):
        hs = slice(h * D_HEAD, (h + 1) * D_HEAD)
        q_ref[:, hs] = l2n(conv_block(h * D_HEAD), D_HEAD ** -0.5).astype(BF16)
        k_ref[:, hs] = l2n(conv_block(D_MODEL + h * D_HEAD), 1.0).astype(BF16)
        v_ref[:, hs] = conv_block(2 * D_MODEL + h * D_HEAD).astype(BF16)

    hb = hb_ext[SUBLANES:SUBLANES + tm]
    za = jnp.dot(hb, wza_ref[...], preferred_element_type=F32)
    sza_ref[...] = za * _sigmoid(za)
    xb_ref[...] = jnp.dot(hb, wxb_ref[...], preferred_element_type=F32)
    zb = jnp.dot(hb, wzb_ref[...], preferred_element_type=F32)
    szb_ref[...] = zb * _sigmoid(zb)
    gc = jnp.dot(hb, wgc_ref[...], preferred_element_type=F32)
    lane = lax.broadcasted_iota(jnp.int32, gc.shape, 1)
    is_decay = (lane < 8) | ((lane >= 16) & (lane < 24))
    log_decay = -jnp.exp(alog_ref[...]) * _softplus(gc + dtb_ref[...])
    gc_ref[...] = jnp.where(is_decay, log_decay, _sigmoid(gc))


def _stage_a_call(x, mod, norm_w, wqkv, wza, wxb, wzb, wgc, alog_vec, dtb_vec, conv_w, conv_b,
                  seq_tiles, per_seq_mod, row0):
    n = x.shape[0]
    tm = TM_A
    blocks8 = tm // SUBLANES
    tok = lambda w: pl.BlockSpec((tm, w), lambda i: (i, 0))
    return pl.pallas_call(
        functools.partial(_stage_a_kernel, seq_tiles, per_seq_mod, row0),
        grid=(n // tm,),
        in_specs=[
            tok(D_MODEL),
            pl.BlockSpec((SUBLANES, D_MODEL), lambda i: (jnp.maximum(i * blocks8 - 1, 0), 0)),
            pl.BlockSpec((SUBLANES, D_MODEL),
                         lambda i: (jnp.minimum((i + 1) * blocks8, n // SUBLANES - 1), 0)),
            _const_spec((SUBLANES, 3 * D_MODEL)),
            _const_spec((1, D_MODEL)),
            _const_spec((D_MODEL, 3 * D_MODEL)),
            _const_spec((D_MODEL, D_MODEL)),
            _const_spec((D_MODEL, D_MODEL)),
            _const_spec((D_MODEL, D_MODEL)),
            _const_spec((D_MODEL, LANES)),
            _const_spec((1, LANES)),
            _const_spec((1, LANES)),
            _const_spec((4, 3 * D_MODEL)),
            _const_spec((1, 3 * D_MODEL)),
        ],
        out_specs=[tok(D_MODEL), tok(D_MODEL), tok(D_MODEL), tok(D_MODEL), tok(D_MODEL),
                   tok(D_MODEL), tok(LANES)],
        out_shape=[
            jax.ShapeDtypeStruct((n, D_MODEL), BF16),
            jax.ShapeDtypeStruct((n, D_MODEL), BF16),
            jax.ShapeDtypeStruct((n, D_MODEL), BF16),
            jax.ShapeDtypeStruct((n, D_MODEL), F32),
            jax.ShapeDtypeStruct((n, D_MODEL), F32),
            jax.ShapeDtypeStruct((n, D_MODEL), F32),
            jax.ShapeDtypeStruct((n, LANES), F32),
        ],
        scratch_shapes=[pltpu.VMEM((tm + 2 * SUBLANES, 3 * D_MODEL), F32)],
        compiler_params=_params(("arbitrary",)),
        name="stage_a",
    )(x, x, x, mod, norm_w, wqkv, wza, wxb, wzb, wgc, alog_vec, dtb_vec, conv_w, conv_b)


def _chunk_cumsum(x, reverse):
    n = x.shape[0]
    row = lax.broadcasted_iota(jnp.int32, x.shape, 0)
    s = 1
    while s < n:
        if reverse:
            x = x + jnp.where(row < n - s, pltpu.roll(x, n - s, 0), 0.0)
        else:
            x = x + jnp.where(row >= s, pltpu.roll(x, s, 0), 0.0)
        s *= 2
    return x


def _run_interleaved(gens):
    gens = list(gens)
    while gens:
        alive = []
        for g in gens:
            try:
                next(g)
                alive.append(g)
            except StopIteration:
                pass
        gens = alive


def _delta_rule_tile(reverse, q_ref, k_ref, v_ref, gc_ref, state, o_ref,
                     u_scr, wq_scr, qk_scr, kd_scr, gt_scr):
    ts = q_ref.shape[0]
    nc = ts // CHUNK
    nh = N_HEADS
    npair = nh // 2
    pshape = (CHUNK, 2 * CHUNK)
    lane_i = lax.broadcasted_iota(jnp.int32, pshape, 1)
    left = lane_i < CHUNK
    ri = lax.broadcasted_iota(jnp.int32, pshape, 0)
    ci = lane_i & (CHUNK - 1)
    if reverse:
        ri, ci = ci, ri
    incl = ri >= ci
    strict = ri > ci
    eye = ri == ci
    level_masks = []
    for lg in range(6):
        s = 1 << lg
        same = (ri >> (lg + 1)) == (ci >> (lg + 1))
        level_masks.append(same & ((ri & (2 * s - 1)) >= s) & ((ci & (2 * s - 1)) < s))
    g_off = 16 if reverse else 0
    b_off = 24 if reverse else 8

    pairs = [(c, p) for c in range(nc) for p in range(npair)]
    rows = lambda c: slice(c * CHUNK, (c + 1) * CHUNK)
    pcols = lambda p: slice(2 * p * D_HEAD, (2 * p + 2) * D_HEAD)
    lane = lambda off, h: slice(off + h, off + h + 1)
    halves = lambda m: (m[:, 0:D_HEAD], m[:, D_HEAD:2 * D_HEAD])
    zero_bf = jnp.zeros((CHUNK, D_HEAD), BF16)

    def blockdiag(m):
        return jnp.concatenate([jnp.where(left, m, 0.0), jnp.where(left, 0.0, m)],
                               axis=0).astype(BF16)

    def blockdiag_wide(m0, m1):
        return jnp.concatenate([jnp.concatenate([m0, zero_bf], axis=1),
                                jnp.concatenate([zero_bf, m1], axis=1)], axis=0)

    gcol = [gc_ref[rows(c), :] for c in range(nc)]
    g_cum = [_chunk_cumsum(g, reverse) for g in gcol]
    g_cum_t = [jnp.concatenate([g, g], axis=0).T for g in g_cum]
    last = 0 if reverse else CHUNK - 1
    g_tot = [g[last:last + 1, :] for g in g_cum]
    for c in range(nc):
        gt_scr[c] = jnp.broadcast_to(g_tot[c], (SUBLANES, LANES))
    yield
    g_i1 = [[g_cum[c][:, lane(g_off, 2 * p + e)] for e in range(2)] for c, p in pairs]
    beta = [[gcol[c][:, lane(b_off, 2 * p + e)] for e in range(2)] for c, p in pairs]
    g_i = [jnp.where(left, a, b) for a, b in g_i1]
    g_j = [jnp.where(left[0:1], g_cum_t[c][lane(g_off, 2 * p), :],
                     g_cum_t[c][lane(g_off, 2 * p + 1), :]) for c, p in pairs]
    k = [halves(k_ref[rows(c), pcols(p)].astype(F32)) for c, p in pairs]
    q = [halves(q_ref[rows(c), pcols(p)].astype(F32)) for c, p in pairs]
    kb = [(k_[0] * b[0], k_[1] * b[1]) for k_, b in zip(k, beta)]
    prod = [lax.dot_general(
        jnp.concatenate([jnp.concatenate(kb_, axis=1), jnp.concatenate(q_, axis=1)],
                        axis=0).astype(BF16),
        blockdiag_wide(k_[0].astype(BF16), k_[1].astype(BF16)),
        (((1,), (1,)), ((), ())), preferred_element_type=F32)
        for kb_, q_, k_ in zip(kb, q, k)]
    yield
    decay = [jnp.exp(jnp.where(incl, a - b, -1e30)) for a, b in zip(g_i, g_j)]
    a_mat = [jnp.where(strict, p_[0:CHUNK] * d, 0.0) for p_, d in zip(prod, decay)]
    for i, (p_, d) in enumerate(zip(prod, decay)):
        qk_scr[i] = (p_[CHUNK:2 * CHUNK] * d).astype(BF16)
    t_inv = [jnp.where(eye, 1.0, 0.0) - jnp.where(level_masks[0], a, 0.0) for a in a_mat]
    yield
    for lvl in range(1, 6):
        x = [jnp.dot(jnp.where(level_masks[lvl], a, 0.0).astype(BF16), blockdiag(t),
                     preferred_element_type=F32) for a, t in zip(a_mat, t_inv)]
        yield
        t_inv = [t - jnp.dot(t.astype(BF16), blockdiag(x_), preferred_element_type=F32)
                 for t, x_ in zip(t_inv, x)]
        yield
    t_off = [jnp.where(eye, 0.0, t).astype(BF16) for t in t_inv]
    e_g = [[jnp.exp(g) for g in gp] for gp in g_i1]
    v = [halves(v_ref[rows(c), pcols(p)].astype(F32)) for c, p in pairs]
    for i, (c, p) in enumerate(pairs):
        for e in range(2):
            rhs = jnp.concatenate([v[i][e] * beta[i][e], kb[i][e] * e_g[i][e]], axis=1)
            rhs_bf = rhs.astype(BF16)
            zeros = jnp.zeros_like(rhs_bf)
            stacked = jnp.concatenate([rhs_bf, zeros] if e == 0 else [zeros, rhs_bf], axis=0)
            sol = rhs + jnp.dot(t_off[i], stacked, preferred_element_type=F32)
            j = 2 * i + e
            u_scr[j] = sol[:, 0:D_HEAD]
            wq_scr[j] = jnp.concatenate([sol[:, D_HEAD:2 * D_HEAD], q[i][e] * e_g[i][e]],
                                        axis=0).astype(BF16)
            kd_scr[j] = (k[i][e] * jnp.exp(g_tot[c][:, lane(g_off, 2 * p + e)]
                                           - g_i1[i][e])).astype(BF16)
    yield

    for c in (range(nc - 1, -1, -1) if reverse else range(nc)):
        idx = [c * nh + h for h in range(nh)]
        s_old = [state[h] for h in range(nh)]
        ws_qs = [jnp.dot(wq_scr[i], s.astype(BF16), preferred_element_type=F32)
                 for i, s in zip(idx, s_old)]
        yield
        v_new = [(u_scr[i] - r[0:CHUNK]).astype(BF16) for i, r in zip(idx, ws_qs)]
        for p in range(npair):
            h0, h1 = 2 * p, 2 * p + 1
            o_ref[rows(c), pcols(p)] = jnp.concatenate(
                [ws_qs[h0][CHUNK:2 * CHUNK], ws_qs[h1][CHUNK:2 * CHUNK]], axis=1) + jnp.dot(
                qk_scr[c * npair + p], blockdiag_wide(v_new[h0], v_new[h1]),
                preferred_element_type=F32)
        yield
        for h in range(nh):
            e_tot = jnp.exp(gt_scr[c][0:1, lane(g_off, h)])
            state[h] = s_old[h] * e_tot + lax.dot_general(
                kd_scr[idx[h]], v_new[h], (((0,), (0,)), ((), ())), preferred_element_type=F32)
        yield


def _gdn_scratch(ts):
    nprob = (ts // CHUNK) * N_HEADS
    return [
        pltpu.VMEM((N_HEADS, D_HEAD, D_HEAD), F32),
        pltpu.VMEM((nprob, CHUNK, D_HEAD), F32),
        pltpu.VMEM((nprob, 2 * CHUNK, D_HEAD), BF16),
        pltpu.VMEM((nprob // 2, CHUNK, 2 * CHUNK), BF16),
        pltpu.VMEM((nprob, CHUNK, D_HEAD), BF16),
        pltpu.VMEM((ts // CHUNK, SUBLANES, LANES), F32),
    ]


def _gdn_kernel(nt, qf_ref, kf_ref, vf_ref, gcf_ref, qb_ref, kb_ref, vb_ref, gcb_ref, s0f_ref, s0b_ref,
                of_ref, ob_ref, sff_ref, sfb_ref, *scr):
    half = len(scr) // 2
    scr_f, scr_b = scr[:half], scr[half:]
    t = pl.program_id(1)

    @pl.when(t == 0)
    def _init():
        scr_f[0][...] = s0f_ref[0]
        scr_b[0][...] = s0b_ref[0]

    _run_interleaved([
        _delta_rule_tile(False, qf_ref, kf_ref, vf_ref, gcf_ref, scr_f[0], of_ref, *scr_f[1:]),
        _delta_rule_tile(True, qb_ref, kb_ref, vb_ref, gcb_ref, scr_b[0], ob_ref, *scr_b[1:]),
    ])

    @pl.when(t == nt - 1)
    def _fin():
        sff_ref[0] = scr_f[0][...]
        sfb_ref[0] = scr_b[0][...]


def _gdn_call(q, k, v, gcols, s0f, s0b, batch):
    n = q.shape[0]
    ts = TS_GDN
    nt = n // batch // ts
    state_spec = pl.BlockSpec((1, N_HEADS, D_HEAD, D_HEAD), lambda b, t: (b, 0, 0, 0))
    tok_f = pl.BlockSpec((ts, D_MODEL), lambda b, t: (b * nt + t, 0))
    tok_b = pl.BlockSpec((ts, D_MODEL), lambda b, t: (b * nt + nt - 1 - t, 0))
    gc_f = pl.BlockSpec((ts, LANES), lambda b, t: (b * nt + t, 0))
    gc_b = pl.BlockSpec((ts, LANES), lambda b, t: (b * nt + nt - 1 - t, 0))
    state_shape = jax.ShapeDtypeStruct((batch, N_HEADS, D_HEAD, D_HEAD), F32)
    return pl.pallas_call(
        functools.partial(_gdn_kernel, nt),
        grid=(batch, nt),
        in_specs=[tok_f, tok_f, tok_f, gc_f, tok_b, tok_b, tok_b, gc_b, state_spec, state_spec],
        out_specs=[tok_f, tok_b, state_spec, state_spec],
        out_shape=[
            jax.ShapeDtypeStruct((n, D_MODEL), F32),
            jax.ShapeDtypeStruct((n, D_MODEL), F32),
            state_shape,
            state_shape,
        ],
        scratch_shapes=_gdn_scratch(ts) + _gdn_scratch(ts),
        compiler_params=_params(("arbitrary", "arbitrary")),
        name="gdn",
    )(q, k, v, gcols, q, k, v, gcols, s0f, s0b)


def _lru_gates(xc, wcat_ref, bcat_ref, lam_ref):
    th = jnp.tanh(_bdot(xc, wcat_ref[0]) + bcat_ref[0])
    half_c_sp = (-0.5 * LRU_C) * _softplus(-lam_ref[0])
    half_x = 0.5 * xc
    out = []
    for d in range(2):
        th_r = th[:, (2 * d) * LANES:(2 * d + 1) * LANES]
        th_i = th[:, (2 * d + 1) * LANES:(2 * d + 2) * LANES]
        hcs = half_c_sp[:, d * LANES:(d + 1) * LANES]
        log_a = th_r * hcs + hcs
        a = jnp.exp(log_a)
        one_m_a2 = jnp.tanh(log_a) * (-1.0 - a * a)
        out.append((a, jnp.sqrt(one_m_a2) * ((th_i + 1.0) * half_x)))
    return out


def _ctx_pitch(seq):
    tiles = seq // SUBLANES
    return (tiles + 1 - tiles % 2) * SUBLANES


def _lru_ctx_kernel(seq, xb_ref, szb_ref, cw_ref, cb_ref, wcat_ref, bcat_ref, lam_ref,
                    ob_ref, hfl_ref, hbl_ref, a_f, b_f, a_b, b_b):
    nb = xb_ref.shape[0] // seq
    pitch = _ctx_pitch(seq)
    rowi = lax.broadcasted_iota(jnp.int32, (seq, 1), 0)

    def per_seq(b, carry):
        x = xb_ref[pl.ds(pl.multiple_of(b * seq, seq), seq), :]
        r0 = pl.multiple_of(b * pitch, SUBLANES)
        xm2 = jnp.where(rowi >= 2, pltpu.roll(x, 2, 0), 0.0)
        xm1 = jnp.where(rowi >= 1, pltpu.roll(x, 1, 0), 0.0)
        xp1 = jnp.where(rowi <= seq - 2, pltpu.roll(x, seq - 1, 0), 0.0)
        xc = (cw_ref[0:1, :] * xm2 + cw_ref[1:2, :] * xm1 + cw_ref[2:3, :] * x
              + cw_ref[3:4, :] * xp1 + cb_ref[...])
        (af, bf), (ab, bb) = _lru_gates(xc, wcat_ref, bcat_ref, lam_ref)
        a_f[pl.ds(r0, seq), :] = af
        b_f[pl.ds(r0, seq), :] = bf
        a_b[pl.ds(r0, seq), :] = ab
        b_b[pl.ds(r0, seq), :] = bb
        return carry

    lax.fori_loop(0, nb, per_seq, 0)

    def step(t, carry):
        hf, hb = carry
        rows_f = pl.ds(t, nb, stride=pitch)
        rows_b = pl.ds(seq - 1 - t, nb, stride=pitch)
        hf = a_f[rows_f, :] * hf + b_f[rows_f, :]
        hb = a_b[rows_b, :] * hb + b_b[rows_b, :]
        a_f[rows_f, :] = hf
        a_b[rows_b, :] = hb
        return hf, hb

    zero = jnp.zeros((nb, LANES), F32)
    lax.fori_loop(0, seq, step, (zero, zero))

    def write(b, carry):
        rows = pl.ds(pl.multiple_of(b * seq, seq), seq)
        prow = pl.ds(pl.multiple_of(b * pitch, SUBLANES), seq)
        ob_ref[rows, :] = (a_f[prow, :] + a_b[prow, :]) * szb_ref[rows, :]
        return carry

    lax.fori_loop(0, nb, write, 0)
    for b in range(nb):
        hfl_ref[b] = a_f[b * pitch + seq - 1:b * pitch + seq, :]
        hbl_ref[b] = a_b[b * pitch:b * pitch + 1, :]


def _lru_lat_kernel(xb_ref, szb_ref, cw_ref, cb_ref, wcat_ref, bcat_ref, lam_ref, h0f_ref, h0b_ref,
                    ob_ref, hfl_ref, hbl_ref, xext, a_f, b_f, a_b, b_b, hin_f, hin_b):
    gw = GRID_W
    n = xb_ref.shape[0]
    nr = n // gw
    wi = lax.broadcasted_iota(jnp.int32, (gw, 1), 0)
    xext[2 * gw:2 * gw + n, :] = xb_ref[...]
    xext[0:gw, :] = jnp.where(wi >= 1, pltpu.roll(xb_ref[(nr - 2) * gw:(nr - 1) * gw, :], 1, 0), 0.0)
    xext[gw:2 * gw, :] = jnp.where(wi >= 1, pltpu.roll(xb_ref[(nr - 1) * gw:nr * gw, :], 1, 0), 0.0)
    xext[2 * gw + n:3 * gw + n, :] = jnp.where(wi <= gw - 2, pltpu.roll(xb_ref[0:gw, :], gw - 1, 0), 0.0)

    rows_per = 256

    def gates(q, carry):
        r0 = pl.multiple_of(q * rows_per, rows_per)
        xc = cb_ref[...] + cw_ref[0:1, :] * xext[pl.ds(r0, rows_per), :]
        for i in range(1, 4):
            xc = xc + cw_ref[i:i + 1, :] * xext[pl.ds(pl.multiple_of(r0 + i * gw, gw), rows_per), :]
        (af, bf), (ab, bb) = _lru_gates(xc, wcat_ref, bcat_ref, lam_ref)
        a_f[pl.ds(r0, rows_per), :] = af
        b_f[pl.ds(r0, rows_per), :] = bf
        a_b[pl.ds(r0, rows_per), :] = ab
        b_b[pl.ds(r0, rows_per), :] = bb
        return carry

    lax.fori_loop(0, n // rows_per, gates, 0)

    ng = gw // SUBLANES

    def col_scan(r, carry):
        acf, bcf, acb, bcb = carry
        rb = nr - 1 - r
        nacf, nbcf, nacb, nbcb = [], [], [], []
        for g in range(ng):
            rows_f = pl.ds(pl.multiple_of(r * gw + g * SUBLANES, SUBLANES), SUBLANES)
            rows_b = pl.ds(pl.multiple_of(rb * gw + g * SUBLANES, SUBLANES), SUBLANES)
            af = a_f[rows_f, :]
            ab = a_b[rows_b, :]
            caf = af * acf[g]
            cbf = af * bcf[g] + b_f[rows_f, :]
            cab = ab * acb[g]
            cbb = ab * bcb[g] + b_b[rows_b, :]
            a_f[rows_f, :] = caf
            b_f[rows_f, :] = cbf
            a_b[rows_b, :] = cab
            b_b[rows_b, :] = cbb
            nacf.append(caf)
            nbcf.append(cbf)
            nacb.append(cab)
            nbcb.append(cbb)
        return tuple(nacf), tuple(nbcf), tuple(nacb), tuple(nbcb)

    one = tuple(jnp.ones((SUBLANES, LANES), F32) for _ in range(ng))
    zero = tuple(jnp.zeros((SUBLANES, LANES), F32) for _ in range(ng))
    lax.fori_loop(0, nr, col_scan, (one, zero, one, zero))

    def carry_cols(w, carry):
        hf, hb = carry
        wb = gw - 1 - w
        hin_f[pl.ds(w, 1), :] = hf
        hin_b[pl.ds(wb, 1), :] = hb
        hf = a_f[pl.ds((nr - 1) * gw + w, 1), :] * hf + b_f[pl.ds((nr - 1) * gw + w, 1), :]
        hb = a_b[pl.ds(wb, 1), :] * hb + b_b[pl.ds(wb, 1), :]
        return hf, hb

    hf, hb = lax.fori_loop(0, gw, carry_cols, (h0f_ref[0], h0b_ref[0]))
    hfl_ref[0] = hf
    hbl_ref[0] = hb

    def finish(r, carry):
        rows = pl.ds(pl.multiple_of(r * gw, gw), gw)
        hsum = (a_f[rows, :] * hin_f[...] + b_f[rows, :]) + (a_b[rows, :] * hin_b[...] + b_b[rows, :])
        ob_ref[rows, :] = hsum * szb_ref[rows, :]
        return carry

    lax.fori_loop(0, nr, finish, 0)


def _lru_common_specs(n, blk):
    col = lambda *ids: ids[-1]
    return [
        pl.BlockSpec((n, LANES), blk),
        pl.BlockSpec((n, LANES), blk),
        pl.BlockSpec((4, LANES), lambda *ids: (0, col(*ids))),
        pl.BlockSpec((1, LANES), lambda *ids: (0, col(*ids))),
        pl.BlockSpec((1, LANES, 4 * LANES), lambda *ids: (col(*ids), 0, 0)),
        pl.BlockSpec((1, 1, 4 * LANES), lambda *ids: (col(*ids), 0, 0)),
        pl.BlockSpec((1, 1, 2 * LANES), lambda *ids: (col(*ids), 0, 0)),
    ]


def _lru_ctx_call(xb, szb, cw, cb, wcat, bcat, lam, batch):
    n = xb.shape[0]
    seq = n // batch
    nblk = D_MODEL // LANES
    st_spec = pl.BlockSpec((batch, 1, LANES), lambda j: (0, 0, j))
    return pl.pallas_call(
        functools.partial(_lru_ctx_kernel, seq),
        grid=(nblk,),
        in_specs=_lru_common_specs(n, lambda j: (0, j)),
        out_specs=[pl.BlockSpec((n, LANES), lambda j: (0, j)), st_spec, st_spec],
        out_shape=[
            jax.ShapeDtypeStruct((n, D_MODEL), F32),
            jax.ShapeDtypeStruct((batch, 1, D_MODEL), F32),
            jax.ShapeDtypeStruct((batch, 1, D_MODEL), F32),
        ],
        scratch_shapes=[pltpu.VMEM((batch * _ctx_pitch(seq), LANES), F32) for _ in range(4)],
        compiler_params=_params(("arbitrary",)),
        name="lru_ctx",
    )(xb, szb, cw, cb, wcat, bcat, lam)


def _lru_lat_call(xb, szb, cw, cb, wcat, bcat, lam, h0f, h0b, batch):
    n = xb.shape[0] // batch
    nblk = D_MODEL // LANES
    st_spec = pl.BlockSpec((1, 1, LANES), lambda b, j: (b, 0, j))
    return pl.pallas_call(
        _lru_lat_kernel,
        grid=(batch, nblk),
        in_specs=_lru_common_specs(n, lambda b, j: (b, j)) + [st_spec, st_spec],
        out_specs=[pl.BlockSpec((n, LANES), lambda b, j: (b, j)), st_spec, st_spec],
        out_shape=[
            jax.ShapeDtypeStruct((batch * n, D_MODEL), F32),
            jax.ShapeDtypeStruct((batch, 1, D_MODEL), F32),
            jax.ShapeDtypeStruct((batch, 1, D_MODEL), F32),
        ],
        scratch_shapes=[pltpu.VMEM((n + 3 * GRID_W, LANES), F32)]
        + [pltpu.VMEM((n, LANES), F32) for _ in range(4)]
        + [pltpu.VMEM((GRID_W, LANES), F32) for _ in range(2)],
        compiler_params=_params(("arbitrary", "arbitrary")),
        name="lru_lat",
    )(xb, szb, cw, cb, wcat, bcat, lam, h0f, h0b)


def _stage_c_kernel(tiles_per_seq, row0, x_ref, mod_ref, nw_ref, of_ref, ob_ref, sza_ref, olru_ref,
                    wg_ref, bg_ref, wpa_ref, wpb_ref, wo_ref, onw_ref, fnw_ref, y_ref):
    row = row0 + pl.program_id(0) // tiles_per_seq
    x = x_ref[...]
    h, gate = _modulated_norm(x, mod_ref, nw_ref, row)
    g = _sigmoid(_bdot(h, wg_ref[...]) + bg_ref[...])
    o = of_ref[...] + ob_ref[...]
    heads = []
    for hh in range(N_HEADS):
        oh = o[:, hh * D_HEAD:(hh + 1) * D_HEAD]
        heads.append(oh * lax.rsqrt(jnp.mean(oh * oh, axis=-1, keepdims=True) + EPS) * onw_ref[...])
    o_a = jnp.concatenate(heads, axis=1) * sza_ref[...]
    pa = _bdot(o_a, wpa_ref[...])
    pb = _bdot(olru_ref[...], wpb_ref[...])
    mixed = _bdot(g[:, 0:D_MODEL] * pa + g[:, D_MODEL:2 * D_MODEL] * pb, wo_ref[...])
    r = x + gate * mixed
    y_ref[...] = r * lax.rsqrt(jnp.mean(r * r, axis=-1, keepdims=True) + EPS) * fnw_ref[...]


def _stage_c_call(x, mod, norm_w, o_f, o_b, sza, olru, wg, bg, wpa, wpb, wo, onw, fnw,
                  tiles_per_seq, row0):
    n = x.shape[0]
    tm = TM_C
    tok = pl.BlockSpec((tm, D_MODEL), lambda i: (i, 0))
    return pl.pallas_call(
        functools.partial(_stage_c_kernel, tiles_per_seq, row0),
        grid=(n // tm,),
        in_specs=[
            tok,
            _const_spec((SUBLANES, 3 * D_MODEL)),
            _const_spec((1, D_MODEL)),
            tok, tok, tok, tok,
            _const_spec((D_MODEL, 2 * D_MODEL)),
            _const_spec((1, 2 * D_MODEL)),
            _const_spec((D_MODEL, D_MODEL)),
            _const_spec((D_MODEL, D_MODEL)),
            _const_spec((D_MODEL, D_MODEL)),
            _const_spec((1, D_HEAD)),
            _const_spec((1, D_MODEL)),
        ],
        out_specs=tok,
        out_shape=jax.ShapeDtypeStruct((n, D_MODEL), F32),
        compiler_params=_params(("arbitrary",)),
        name="stage_c",
    )(x, mod, norm_w, o_f, o_b, sza, olru, wg, bg, wpa, wpb, wo, onw, fnw)


def kernel(x_prompt, x_sample, state_a_fwd, state_a_bwd, state_b_fwd, state_b_bwd, c, c_ctx,
           norm_w, w_mod, b_mod, w_in, conv_a_w, conv_a_b, a_log_fwd, dt_bias_fwd, a_log_bwd,
           dt_bias_bwd, onorm_a_w, conv_b_w, conv_b_b, lru_wa_fwd, lru_ba_fwd, lru_wx_fwd,
           lru_bx_fwd, lru_lambda_fwd, lru_wa_bwd, lru_ba_bwd, lru_wx_bwd, lru_bx_bwd,
           lru_lambda_bwd, w_proj_a, w_proj_b, w_gate, b_gate, w_out, final_norm_w):
    bp, seq, d = x_prompt.shape
    bd, dseq, _ = x_sample.shape
    l = 0
    d3 = 3 * D_MODEL

    w_in_l = w_in[l]
    wqkv = w_in_l[:, 0:d3].astype(BF16)
    wza = w_in_l[:, d3:d3 + D_MODEL].astype(BF16)
    wxb = w_in_l[:, d3 + D_MODEL:d3 + 2 * D_MODEL].astype(BF16)
    wzb = w_in_l[:, d3 + 2 * D_MODEL:d3 + 3 * D_MODEL].astype(BF16)
    n_gate_cols = 4 * N_HEADS
    wgc = jnp.pad(w_in_l[:, d3 + 3 * D_MODEL:], ((0, 0), (0, LANES - n_gate_cols))).astype(BF16)
    zeros8 = jnp.zeros((N_HEADS,), F32)
    pad_lanes = jnp.zeros((LANES - n_gate_cols,), F32)
    alog_vec = jnp.concatenate([a_log_fwd[l], zeros8, a_log_bwd[l], zeros8, pad_lanes])[None, :]
    dtb_vec = jnp.concatenate([dt_bias_fwd[l], zeros8, dt_bias_bwd[l], zeros8, pad_lanes])[None, :]
    nw = norm_w[l][None, :]
    cvec = jnp.concatenate([c_ctx[None, :], c, jnp.zeros((SUBLANES - 1 - bd, d), F32)], axis=0)
    conv_aw = conv_a_w[l]
    conv_ab = conv_a_b[l][None, :]
    conv_bw = conv_b_w[l]
    conv_bb = conv_b_b[l][None, :]
    wcat = (0.5 * jnp.concatenate([lru_wa_fwd[l], lru_wx_fwd[l], lru_wa_bwd[l], lru_wx_bwd[l]],
                                  axis=2)).astype(BF16)
    nblk = D_MODEL // LANES
    blk = lambda v: v.reshape(nblk, 1, LANES)
    bcat = 0.5 * jnp.concatenate([blk(lru_ba_fwd[l]), blk(lru_bx_fwd[l]), blk(lru_ba_bwd[l]),
                                  blk(lru_bx_bwd[l])], axis=2)
    lam = jnp.concatenate([blk(lru_lambda_fwd[l]), blk(lru_lambda_bwd[l])], axis=2)
    wg = w_gate[l].astype(BF16)
    bg = b_gate[l][None, :]
    wpa = w_proj_a[l].astype(BF16)
    wpb = w_proj_b[l].astype(BF16)
    wo = w_out[l].astype(BF16)
    onw = onorm_a_w[l][None, :]
    fnw = final_norm_w[None, :]

    mod = _mod_call(cvec, w_mod[l], b_mod[l][None, :])

    def path(x2d, batch, mod_row0, per_seq_mod, tiles_per_seq_c, s_af, s_ab, lru_fn):
        seq_tiles = x2d.shape[0] // batch // TM_A
        qn, kn, vn, sza, xb, szb, gcols = _stage_a_call(
            x2d, mod, nw, wqkv, wza, wxb, wzb, wgc, alog_vec, dtb_vec, conv_aw, conv_ab,
            seq_tiles, per_seq_mod, mod_row0)
        o_f, o_b, saf, sab = _gdn_call(qn, kn, vn, gcols, s_af, s_ab, batch)
        olru, sbf, sbb = lru_fn(xb, szb)
        y = _stage_c_call(x2d, mod, nw, o_f, o_b, sza, olru, wg, bg, wpa, wpb, wo, onw, fnw,
                          tiles_per_seq_c, mod_row0)
        return y, saf, sab, sbf, sbb

    n_ctx = bp * seq
    zero_state = jnp.zeros((bp, N_HEADS, D_HEAD, D_HEAD), F32)
    yp, saf, sab, sbf, sbb = path(
        x_prompt.reshape(n_ctx, d), bp, 0, False, n_ctx // TM_C, zero_state, zero_state,
        lambda xb, szb: _lru_ctx_call(xb, szb, conv_bw, conv_bb, wcat, bcat, lam, bp))

    ys, _, _, _, _ = path(
        x_sample.reshape(bd * dseq, d), bd, 1, True, dseq // TM_C,
        state_a_fwd[:, l], state_a_bwd[:, l],
        lambda xb, szb: _lru_lat_call(xb, szb, conv_bw, conv_bb, wcat, bcat, lam,
                                      state_b_fwd[:, l][:, None, :], state_b_bwd[:, l][:, None, :], bd))

    return (yp.reshape(bp, seq, d), ys.reshape(bd, dseq, d),
            saf[:, None], sab[:, None], sbf, sbb)
```

```python
import functools

import jax
import jax.numpy as jnp
from jax import lax
from jax.experimental import pallas as pl
from jax.experimental.pallas import tpu as pltpu

F32 = jnp.float32
BF16 = jnp.bfloat16
HIGHEST = lax.Precision.HIGHEST

D_MODEL = 1024
N_HEADS = 8
D_HEAD = 128
CHUNK = 64
GRID_W = 64
LRU_C = 8.0
EPS = 1e-6
LANES = 128
SUBLANES = 8
VMEM_LIMIT_BYTES = 56 * 1024 * 1024

TM_A = 256
TM_C = 256
TS_GDN = 256


def _bdot(a, b):
    return jnp.dot(a.astype(BF16), b.astype(BF16), preferred_element_type=F32)


def _sigmoid(x):
    return 1.0 / (1.0 + jnp.exp(-x))


def _softplus(x):
    return jnp.maximum(x, 0.0) + jnp.log1p(jnp.exp(-jnp.abs(x)))


def _sigmoid_t(x):
    return 0.5 * jnp.tanh(0.5 * x) + 0.5


def _const_spec(shape):
    n = len(shape)
    return pl.BlockSpec(shape, lambda *_: (0,) * n)


def _params(sem, vmem=VMEM_LIMIT_BYTES):
    return pltpu.CompilerParams(dimension_semantics=sem, vmem_limit_bytes=vmem)


def _mod_kernel(c_ref, w_ref, b_ref, o_ref):
    c = c_ref[...]
    o_ref[...] = _bdot(c * _sigmoid(c), w_ref[...]) + b_ref[...]


def _mod_call(cvec, w_mod, b_mod):
    n_col = 3
    return pl.pallas_call(
        _mod_kernel,
        grid=(n_col,),
        in_specs=[
            _const_spec((SUBLANES, D_MODEL)),
            pl.BlockSpec((D_MODEL, D_MODEL), lambda j: (0, j)),
            pl.BlockSpec((1, D_MODEL), lambda j: (0, j)),
        ],
        out_specs=pl.BlockSpec((SUBLANES, D_MODEL), lambda j: (0, j)),
        out_shape=jax.ShapeDtypeStruct((SUBLANES, 3 * D_MODEL), F32),
        compiler_params=_params(("arbitrary",)),
        name="mod_vectors",
    )(cvec, w_mod, b_mod)


def _chunk_transposed(ref):
    tm = ref.shape[0]
    return jnp.concatenate(
        [ref[pl.ds(c * CHUNK + a, SUBLANES, stride=SUBLANES), :]
         for c in range(tm // CHUNK) for a in range(CHUNK // SUBLANES)], axis=0)


def _chunk_time(pos):
    assert CHUNK == SUBLANES * SUBLANES
    shift = SUBLANES.bit_length() - 1
    return ((pos & (SUBLANES - 1)) << shift) + (pos >> shift)


def _modulated_norm(x, mod_ref, nw_ref, row):
    mod = mod_ref[pl.ds(row, 1), :]
    shift = mod[:, 0:D_MODEL]
    scale = mod[:, D_MODEL:2 * D_MODEL]
    gate = mod[:, 2 * D_MODEL:3 * D_MODEL]
    y = x * lax.rsqrt(jnp.mean(x * x, axis=-1, keepdims=True) + EPS) * nw_ref[...]
    return y * (1.0 + scale) + shift, gate


def _stage_a_kernel(seq_tiles, per_seq_mod, row0, x_ref, *refs):
    ncol = D_MODEL // LANES
    xcol_refs = refs[:ncol]
    (xp_ref, xn_ref, mod_ref, nw_ref, wqkv_ref, wza_ref, wxb_ref, wzb_ref, wgc_ref, alog_ref,
     dtb_ref, cw_ref, cb_ref, q_ref, k_ref, v_ref, sza_ref, xb_ref, szb_ref, gc_ref) = refs[ncol:]
    tm = x_ref.shape[0]
    nck = tm // CHUNK
    i = pl.program_id(0)
    ti = i % seq_tiles
    row = row0 + (i // seq_tiles if per_seq_mod else 0)

    x_t = jnp.concatenate([_chunk_transposed(r) for r in xcol_refs], axis=1)
    x_ext = jnp.concatenate([xp_ref[...], x_t, xn_ref[...]], axis=0)
    h_ext, _ = _modulated_norm(x_ext, mod_ref, nw_ref, row)
    hb_ext = h_ext.astype(BF16)
    sub = lax.broadcasted_iota(jnp.int32, (SUBLANES, D_HEAD), 0)
    down = lambda m, n=1: pltpu.roll(m, n, 0)
    up = lambda m: pltpu.roll(m, SUBLANES - 1, 0)

    def conv_block(proj, col, pcol):
        cs = slice(col, col + D_HEAD)
        w = [cw_ref[t:t + 1, cs] for t in range(4)]
        proj = proj[:, pcol:pcol + D_HEAD]
        xs = [proj[SUBLANES + c * CHUNK:SUBLANES + (c + 1) * CHUNK] for c in range(nck)]
        before = jnp.where(ti == 0, 0.0, proj[0:SUBLANES])
        after = jnp.where(ti == seq_tiles - 1, 0.0,
                          proj[SUBLANES + tm:2 * SUBLANES + tm])
        out = []
        for c in range(nck):
            x = xs[c]
            m1_prev = down(xs[c - 1][CHUNK - 8:CHUNK]) if c > 0 else down(before)
            m2_prev = down(xs[c - 1][CHUNK - 16:CHUNK - 8]) if c > 0 else down(before, 2)
            p1_next = up(xs[c + 1][0:8]) if c < nck - 1 else up(after)
            d7 = jnp.where(sub == 0, m1_prev, down(x[CHUNK - 8:CHUNK]))
            d6 = jnp.where(sub == 0, m2_prev, down(x[CHUNK - 16:CHUNK - 8]))
            u0 = jnp.where(sub == SUBLANES - 1, p1_next, up(x[0:8]))
            xm1 = jnp.concatenate([d7, x[0:CHUNK - 8]], axis=0)
            xm2 = jnp.concatenate([d6, d7, x[0:CHUNK - 16]], axis=0)
            xp1 = jnp.concatenate([x[8:CHUNK], u0], axis=0)
            acc = cb_ref[:, cs] + w[0] * xm2 + w[1] * xm1 + w[2] * x + w[3] * xp1
            out.append(acc * _sigmoid(acc))
        return jnp.concatenate(out, axis=0)

    def l2n(v, scale):
        return v * (lax.rsqrt(jnp.sum(v * v, axis=-1, keepdims=True) + EPS) * scale)

    h_nat, _ = _modulated_norm(x_ref[...], mod_ref, nw_ref, row)
    hb = h_nat.astype(BF16)

    def other_branch(j):
        if j == 0:
            za = jnp.dot(hb, wza_ref[...], preferred_element_type=F32)
            sza_ref[...] = za * _sigmoid(za)
        elif j == 1:
            xb_ref[...] = jnp.dot(hb, wxb_ref[...], preferred_element_type=F32)
        else:
            zb = jnp.dot(hb, wzb_ref[...], preferred_element_type=F32)
            szb_ref[...] = zb * _sigmoid(zb)

    post = [lambda t: l2n(t, D_HEAD ** -0.5), lambda t: l2n(t, 1.0), lambda t: t]
    for j, out_ref in enumerate((q_ref, k_ref, v_ref)):
        proj = jnp.dot(hb_ext, wqkv_ref[:, j * D_MODEL:(j + 1) * D_MODEL],
                       preferred_element_type=F32)
        for h in range(N_HEADS):
            hs = slice(h * D_HEAD, (h + 1) * D_HEAD)
            out_ref[:, hs] = post[j](conv_block(proj, j * D_MODEL + h * D_HEAD,
                                                h * D_HEAD)).astype(BF16)
        other_branch(j)

    gc = jnp.dot(hb_ext[SUBLANES:SUBLANES + tm], wgc_ref[...], preferred_element_type=F32)
    lane = lax.broadcasted_iota(jnp.int32, gc.shape, 1)
    is_decay = (lane < 8) | ((lane >= 16) & (lane < 24))
    log_decay = -jnp.exp(alog_ref[...]) * _softplus(gc + dtb_ref[...])
    gc_ref[...] = jnp.where(is_decay, log_decay, _sigmoid(gc))


def _stage_a_call(x, mod, norm_w, wqkv, wza, wxb, wzb, wgc, alog_vec, dtb_vec, conv_w, conv_b,
                  seq_tiles, per_seq_mod, row0):
    n = x.shape[0]
    tm = TM_A
    blocks8 = tm // SUBLANES
    ncol = D_MODEL // LANES
    tok = lambda w: pl.BlockSpec((tm, w), lambda i: (i, 0))
    return pl.pallas_call(
        functools.partial(_stage_a_kernel, seq_tiles, per_seq_mod, row0),
        grid=(n // tm,),
        in_specs=[tok(D_MODEL)]
        + [pl.BlockSpec((tm, LANES), lambda i, j=j: (i, j)) for j in range(ncol)]
        + [
            pl.BlockSpec((SUBLANES, D_MODEL), lambda i: (jnp.maximum(i * blocks8 - 1, 0), 0)),
            pl.BlockSpec((SUBLANES, D_MODEL),
                         lambda i: (jnp.minimum((i + 1) * blocks8, n // SUBLANES - 1), 0)),
            _const_spec((SUBLANES, 3 * D_MODEL)),
            _const_spec((1, D_MODEL)),
            _const_spec((D_MODEL, 3 * D_MODEL)),
            _const_spec((D_MODEL, D_MODEL)),
            _const_spec((D_MODEL, D_MODEL)),
            _const_spec((D_MODEL, D_MODEL)),
            _const_spec((D_MODEL, LANES)),
            _const_spec((1, LANES)),
            _const_spec((1, LANES)),
            _const_spec((4, 3 * D_MODEL)),
            _const_spec((1, 3 * D_MODEL)),
        ],
        out_specs=[tok(D_MODEL), tok(D_MODEL), tok(D_MODEL), tok(D_MODEL), tok(D_MODEL),
                   tok(D_MODEL), tok(LANES)],
        out_shape=[
            jax.ShapeDtypeStruct((n, D_MODEL), BF16),
            jax.ShapeDtypeStruct((n, D_MODEL), BF16),
            jax.ShapeDtypeStruct((n, D_MODEL), BF16),
            jax.ShapeDtypeStruct((n, D_MODEL), F32),
            jax.ShapeDtypeStruct((n, D_MODEL), F32),
            jax.ShapeDtypeStruct((n, D_MODEL), F32),
            jax.ShapeDtypeStruct((n, LANES), F32),
        ],
        compiler_params=_params(("arbitrary",)),
        name="stage_a",
    )(x, *([x] * ncol), x, x, mod, norm_w, wqkv, wza, wxb, wzb, wgc, alog_vec, dtb_vec, conv_w, conv_b)


def _chunk_cumsum(x, reverse):
    nv = CHUNK // SUBLANES
    v = [x[a * SUBLANES:(a + 1) * SUBLANES] for a in range(nv)]
    order = range(nv - 2, -1, -1) if reverse else range(1, nv)
    for a in order:
        v[a] = v[a] + v[a + 1 if reverse else a - 1]
    tot = v[0] if reverse else v[nv - 1]
    sub = lax.broadcasted_iota(jnp.int32, tot.shape, 0)
    acc = tot
    s = 1
    while s < SUBLANES:
        if reverse:
            acc = acc + jnp.where(sub < SUBLANES - s, pltpu.roll(acc, SUBLANES - s, 0), 0.0)
        else:
            acc = acc + jnp.where(sub >= s, pltpu.roll(acc, s, 0), 0.0)
        s *= 2
    if reverse:
        carry = jnp.where(sub < SUBLANES - 1, pltpu.roll(acc, SUBLANES - 1, 0), 0.0)
    else:
        carry = jnp.where(sub >= 1, pltpu.roll(acc, 1, 0), 0.0)
    return jnp.concatenate([p + carry for p in v], axis=0)


def _run_interleaved(gens):
    gens = list(gens)
    while gens:
        alive = []
        for g in gens:
            try:
                next(g)
                alive.append(g)
            except StopIteration:
                pass
        gens = alive


def _delta_rule_tile(reverse, q_ref, k_ref, v_ref, gc_ref, state, o_ref,
                     u_scr, wq_scr, qk_scr, kd_scr, gt_scr):
    ts = q_ref.shape[0]
    nc = ts // CHUNK
    nh = N_HEADS
    npair = nh // 2
    pshape = (CHUNK, 2 * CHUNK)
    lane_i = lax.broadcasted_iota(jnp.int32, pshape, 1)
    left = lane_i < CHUNK
    ri = _chunk_time(lax.broadcasted_iota(jnp.int32, pshape, 0))
    ci = _chunk_time(lane_i & (CHUNK - 1))
    if reverse:
        ri, ci = ci, ri
    incl = ri >= ci
    strict = ri > ci
    eye = ri == ci
    level_masks = []
    for lg in range(6):
        s = 1 << lg
        same = (ri >> (lg + 1)) == (ci >> (lg + 1))
        level_masks.append(same & ((ri & (2 * s - 1)) >= s) & ((ci & (2 * s - 1)) < s))
    g_off = 16 if reverse else 0
    b_off = 24 if reverse else 8

    pairs = [(c, p) for c in range(nc) for p in range(npair)]
    rows = lambda c: slice(c * CHUNK, (c + 1) * CHUNK)
    pcols = lambda p: slice(2 * p * D_HEAD, (2 * p + 2) * D_HEAD)
    lane = lambda off, h: slice(off + h, off + h + 1)
    halves = lambda m: (m[:, 0:D_HEAD], m[:, D_HEAD:2 * D_HEAD])
    zero_bf = jnp.zeros((CHUNK, D_HEAD), BF16)

    def blockdiag(m):
        return jnp.concatenate([jnp.where(left, m, 0.0), jnp.where(left, 0.0, m)],
                               axis=0).astype(BF16)

    def blockdiag_wide(m0, m1):
        return jnp.concatenate([jnp.concatenate([m0, zero_bf], axis=1),
                                jnp.concatenate([zero_bf, m1], axis=1)], axis=0)

    gcol = [gc_ref[rows(c), :] for c in range(nc)]
    g_cum = [_chunk_cumsum(g, reverse) for g in gcol]
    g_cum_t = [jnp.concatenate([g, g], axis=0).T for g in g_cum]
    last = 0 if reverse else CHUNK - 1
    g_tot = [g[last:last + 1, :] for g in g_cum]
    for c in range(nc):
        gt_scr[c] = jnp.broadcast_to(g_tot[c], (SUBLANES, LANES))
    yield
    g_i1 = [[g_cum[c][:, lane(g_off, 2 * p + e)] for e in range(2)] for c, p in pairs]
    beta = [[gcol[c][:, lane(b_off, 2 * p + e)] for e in range(2)] for c, p in pairs]
    g_i = [jnp.where(left, a, b) for a, b in g_i1]
    g_j = [jnp.where(left[0:1], g_cum_t[c][lane(g_off, 2 * p), :],
                     g_cum_t[c][lane(g_off, 2 * p + 1), :]) for c, p in pairs]
    k = [halves(k_ref[rows(c), pcols(p)].astype(F32)) for c, p in pairs]
    q = [halves(q_ref[rows(c), pcols(p)].astype(F32)) for c, p in pairs]
    kb = [(k_[0] * b[0], k_[1] * b[1]) for k_, b in zip(k, beta)]
    prod = [lax.dot_general(
        jnp.concatenate([jnp.concatenate(kb_, axis=1), jnp.concatenate(q_, axis=1)],
                        axis=0).astype(BF16),
        blockdiag_wide(k_[0].astype(BF16), k_[1].astype(BF16)),
        (((1,), (1,)), ((), ())), preferred_element_type=F32)
        for kb_, q_, k_ in zip(kb, q, k)]
    yield
    decay = [jnp.exp(jnp.where(incl, a - b, -1e30)) for a, b in zip(g_i, g_j)]
    a_mat = [jnp.where(strict, p_[0:CHUNK] * d, 0.0) for p_, d in zip(prod, decay)]
    for i, (p_, d) in enumerate(zip(prod, decay)):
        qk_scr[i] = (p_[CHUNK:2 * CHUNK] * d).astype(BF16)
    t_inv = [jnp.where(eye, 1.0, 0.0) - jnp.where(level_masks[0], a, 0.0) for a in a_mat]
    yield
    for lvl in range(1, 6):
        x = [jnp.dot(jnp.where(level_masks[lvl], a, 0.0).astype(BF16), blockdiag(t),
                     preferred_element_type=F32) for a, t in zip(a_mat, t_inv)]
        yield
        t_inv = [t - jnp.dot(t.astype(BF16), blockdiag(x_), preferred_element_type=F32)
                 for t, x_ in zip(t_inv, x)]
        yield
    t_off = [jnp.where(eye, 0.0, t).astype(BF16) for t in t_inv]
    e_g = [[jnp.exp(g) for g in gp] for gp in g_i1]
    v = [halves(v_ref[rows(c), pcols(p)].astype(F32)) for c, p in pairs]
    for i, (c, p) in enumerate(pairs):
        for e in range(2):
            rhs = jnp.concatenate([v[i][e] * beta[i][e], kb[i][e] * e_g[i][e]], axis=1)
            rhs_bf = rhs.astype(BF16)
            zeros = jnp.zeros_like(rhs_bf)
            stacked = jnp.concatenate([rhs_bf, zeros] if e == 0 else [zeros, rhs_bf], axis=0)
            sol = rhs + jnp.dot(t_off[i], stacked, preferred_element_type=F32)
            j = 2 * i + e
            u_scr[j] = sol[:, 0:D_HEAD]
            wq_scr[j] = jnp.concatenate([sol[:, D_HEAD:2 * D_HEAD], q[i][e] * e_g[i][e]],
                                        axis=0).astype(BF16)
            kd_scr[j] = (k[i][e] * jnp.exp(g_tot[c][:, lane(g_off, 2 * p + e)]
                                           - g_i1[i][e])).astype(BF16)
    yield

    for c in (range(nc - 1, -1, -1) if reverse else range(nc)):
        idx = [c * nh + h for h in range(nh)]
        s_old = [state[h] for h in range(nh)]
        ws_qs = [jnp.dot(wq_scr[i], s.astype(BF16), preferred_element_type=F32)
                 for i, s in zip(idx, s_old)]
        yield
        v_new = [(u_scr[i] - r[0:CHUNK]).astype(BF16) for i, r in zip(idx, ws_qs)]
        for p in range(npair):
            h0, h1 = 2 * p, 2 * p + 1
            o_ref[rows(c), pcols(p)] = jnp.concatenate(
                [ws_qs[h0][CHUNK:2 * CHUNK], ws_qs[h1][CHUNK:2 * CHUNK]], axis=1) + jnp.dot(
                qk_scr[c * npair + p], blockdiag_wide(v_new[h0], v_new[h1]),
                preferred_element_type=F32)
        yield
        for h in range(nh):
            e_tot = jnp.exp(gt_scr[c][0:1, lane(g_off, h)])
            state[h] = s_old[h] * e_tot + lax.dot_general(
                kd_scr[idx[h]], v_new[h], (((0,), (0,)), ((), ())), preferred_element_type=F32)
        yield


def _gdn_scratch(ts):
    nprob = (ts // CHUNK) * N_HEADS
    return [
        pltpu.VMEM((N_HEADS, D_HEAD, D_HEAD), F32),
        pltpu.VMEM((nprob, CHUNK, D_HEAD), F32),
        pltpu.VMEM((nprob, 2 * CHUNK, D_HEAD), BF16),
        pltpu.VMEM((nprob // 2, CHUNK, 2 * CHUNK), BF16),
        pltpu.VMEM((nprob, CHUNK, D_HEAD), BF16),
        pltpu.VMEM((ts // CHUNK, SUBLANES, LANES), F32),
    ]


def _gdn_kernel(nt, qf_ref, kf_ref, vf_ref, gcf_ref, qb_ref, kb_ref, vb_ref, gcb_ref, s0f_ref, s0b_ref,
                of_ref, ob_ref, sff_ref, sfb_ref, *scr):
    half = len(scr) // 2
    scr_f, scr_b = scr[:half], scr[half:]
    t = pl.program_id(1)

    @pl.when(t == 0)
    def _init():
        scr_f[0][...] = s0f_ref[0]
        scr_b[0][...] = s0b_ref[0]

    _run_interleaved([
        _delta_rule_tile(False, qf_ref, kf_ref, vf_ref, gcf_ref, scr_f[0], of_ref, *scr_f[1:]),
        _delta_rule_tile(True, qb_ref, kb_ref, vb_ref, gcb_ref, scr_b[0], ob_ref, *scr_b[1:]),
    ])

    @pl.when(t == nt - 1)
    def _fin():
        sff_ref[0] = scr_f[0][...]
        sfb_ref[0] = scr_b[0][...]


def _gdn_call(q, k, v, gcols, s0f, s0b, batch):
    n = q.shape[0]
    ts = TS_GDN
    nt = n // batch // ts
    state_spec = pl.BlockSpec((1, N_HEADS, D_HEAD, D_HEAD), lambda b, t: (b, 0, 0, 0))
    tok_f = pl.BlockSpec((ts, D_MODEL), lambda b, t: (b * nt + t, 0))
    tok_b = pl.BlockSpec((ts, D_MODEL), lambda b, t: (b * nt + nt - 1 - t, 0))
    gc_f = pl.BlockSpec((ts, LANES), lambda b, t: (b * nt + t, 0))
    gc_b = pl.BlockSpec((ts, LANES), lambda b, t: (b * nt + nt - 1 - t, 0))
    state_shape = jax.ShapeDtypeStruct((batch, N_HEADS, D_HEAD, D_HEAD), F32)
    return pl.pallas_call(
        functools.partial(_gdn_kernel, nt),
        grid=(batch, nt),
        in_specs=[tok_f, tok_f, tok_f, gc_f, tok_b, tok_b, tok_b, gc_b, state_spec, state_spec],
        out_specs=[tok_f, tok_b, state_spec, state_spec],
        out_shape=[
            jax.ShapeDtypeStruct((n, D_MODEL), F32),
            jax.ShapeDtypeStruct((n, D_MODEL), F32),
            state_shape,
            state_shape,
        ],
        scratch_shapes=_gdn_scratch(ts) + _gdn_scratch(ts),
        compiler_params=_params(("arbitrary", "arbitrary")),
        name="gdn",
    )(q, k, v, gcols, q, k, v, gcols, s0f, s0b)


def _lru_gates(xc, wcat_ref, bcat_ref, lam_ref):
    th = jnp.tanh(_bdot(xc, wcat_ref[0]) + bcat_ref[0])
    half_c_sp = (-0.5 * LRU_C) * _softplus(-lam_ref[0])
    half_x = 0.5 * xc
    out = []
    for d in range(2):
        th_r = th[:, (2 * d) * LANES:(2 * d + 1) * LANES]
        th_i = th[:, (2 * d + 1) * LANES:(2 * d + 2) * LANES]
        hcs = half_c_sp[:, d * LANES:(d + 1) * LANES]
        log_a = th_r * hcs + hcs
        a = jnp.exp(log_a)
        one_m_a2 = jnp.tanh(log_a) * (-1.0 - a * a)
        out.append((a, jnp.sqrt(one_m_a2) * ((th_i + 1.0) * half_x)))
    return out


def _ctx_pitch(seq):
    tiles = seq // SUBLANES
    return (tiles + 1 - tiles % 2) * SUBLANES


def _lru_ctx_kernel(seq, xb_ref, szb_ref, cw_ref, cb_ref, wcat_ref, bcat_ref, lam_ref,
                    ob_ref, hfl_ref, hbl_ref, a_f, b_f, a_b, b_b):
    nb = xb_ref.shape[0] // seq
    pitch = _ctx_pitch(seq)
    rowi = lax.broadcasted_iota(jnp.int32, (seq, 1), 0)

    def per_seq(b, carry):
        x = xb_ref[pl.ds(pl.multiple_of(b * seq, seq), seq), :]
        r0 = pl.multiple_of(b * pitch, SUBLANES)
        xm2 = jnp.where(rowi >= 2, pltpu.roll(x, 2, 0), 0.0)
        xm1 = jnp.where(rowi >= 1, pltpu.roll(x, 1, 0), 0.0)
        xp1 = jnp.where(rowi <= seq - 2, pltpu.roll(x, seq - 1, 0), 0.0)
        xc = (cw_ref[0:1, :] * xm2 + cw_ref[1:2, :] * xm1 + cw_ref[2:3, :] * x
              + cw_ref[3:4, :] * xp1 + cb_ref[...])
        (af, bf), (ab, bb) = _lru_gates(xc, wcat_ref, bcat_ref, lam_ref)
        a_f[pl.ds(r0, seq), :] = af
        b_f[pl.ds(r0, seq), :] = bf
        a_b[pl.ds(r0, seq), :] = ab
        b_b[pl.ds(r0, seq), :] = bb
        return carry

    lax.fori_loop(0, nb, per_seq, 0)

    def step(t, carry):
        hf, hb = carry
        rows_f = pl.ds(t, nb, stride=pitch)
        rows_b = pl.ds(seq - 1 - t, nb, stride=pitch)
        hf = a_f[rows_f, :] * hf + b_f[rows_f, :]
        hb = a_b[rows_b, :] * hb + b_b[rows_b, :]
        a_f[rows_f, :] = hf
        a_b[rows_b, :] = hb
        return hf, hb

    zero = jnp.zeros((nb, LANES), F32)
    lax.fori_loop(0, seq, step, (zero, zero))

    def write(b, carry):
        rows = pl.ds(pl.multiple_of(b * seq, seq), seq)
        prow = pl.ds(pl.multiple_of(b * pitch, SUBLANES), seq)
        ob_ref[rows, :] = (a_f[prow, :] + a_b[prow, :]) * szb_ref[rows, :]
        return carry

    lax.fori_loop(0, nb, write, 0)
    for b in range(nb):
        hfl_ref[b] = a_f[b * pitch + seq - 1:b * pitch + seq, :]
        hbl_ref[b] = a_b[b * pitch:b * pitch + 1, :]


def _lru_lat_kernel(xb_ref, szb_ref, cw_ref, cb_ref, wcat_ref, bcat_ref, lam_ref, h0f_ref, h0b_ref,
                    ob_ref, hfl_ref, hbl_ref, xext, a_f, b_f, a_b, b_b, hin_f, hin_b):
    gw = GRID_W
    n = xb_ref.shape[0]
    nr = n // gw
    wi = lax.broadcasted_iota(jnp.int32, (gw, 1), 0)
    xext[2 * gw:2 * gw + n, :] = xb_ref[...]
    xext[0:gw, :] = jnp.where(wi >= 1, pltpu.roll(xb_ref[(nr - 2) * gw:(nr - 1) * gw, :], 1, 0), 0.0)
    xext[gw:2 * gw, :] = jnp.where(wi >= 1, pltpu.roll(xb_ref[(nr - 1) * gw:nr * gw, :], 1, 0), 0.0)
    xext[2 * gw + n:3 * gw + n, :] = jnp.where(wi <= gw - 2, pltpu.roll(xb_ref[0:gw, :], gw - 1, 0), 0.0)

    rows_per = 256

    def gates(q, carry):
        r0 = pl.multiple_of(q * rows_per, rows_per)
        xc = cb_ref[...] + cw_ref[0:1, :] * xext[pl.ds(r0, rows_per), :]
        for i in range(1, 4):
            xc = xc + cw_ref[i:i + 1, :] * xext[pl.ds(pl.multiple_of(r0 + i * gw, gw), rows_per), :]
        (af, bf), (ab, bb) = _lru_gates(xc, wcat_ref, bcat_ref, lam_ref)
        a_f[pl.ds(r0, rows_per), :] = af
        b_f[pl.ds(r0, rows_per), :] = bf
        a_b[pl.ds(r0, rows_per), :] = ab
        b_b[pl.ds(r0, rows_per), :] = bb
        return carry

    lax.fori_loop(0, n // rows_per, gates, 0)

    ng = gw // SUBLANES

    def col_scan(r, carry):
        acf, bcf, acb, bcb = carry
        rb = nr - 1 - r
        nacf, nbcf, nacb, nbcb = [], [], [], []
        for g in range(ng):
            rows_f = pl.ds(pl.multiple_of(r * gw + g * SUBLANES, SUBLANES), SUBLANES)
            rows_b = pl.ds(pl.multiple_of(rb * gw + g * SUBLANES, SUBLANES), SUBLANES)
            af = a_f[rows_f, :]
            ab = a_b[rows_b, :]
            caf = af * acf[g]
            cbf = af * bcf[g] + b_f[rows_f, :]
            cab = ab * acb[g]
            cbb = ab * bcb[g] + b_b[rows_b, :]
            a_f[rows_f, :] = caf
            b_f[rows_f, :] = cbf
            a_b[rows_b, :] = cab
            b_b[rows_b, :] = cbb
            nacf.append(caf)
            nbcf.append(cbf)
            nacb.append(cab)
            nbcb.append(cbb)
        return tuple(nacf), tuple(nbcf), tuple(nacb), tuple(nbcb)

    one = tuple(jnp.ones((SUBLANES, LANES), F32) for _ in range(ng))
    zero = tuple(jnp.zeros((SUBLANES, LANES), F32) for _ in range(ng))
    lax.fori_loop(0, nr, col_scan, (one, zero, one, zero))

    def carry_cols(w, carry):
        hf, hb = carry
        wb = gw - 1 - w
        hin_f[pl.ds(w, 1), :] = hf
        hin_b[pl.ds(wb, 1), :] = hb
        hf = a_f[pl.ds((nr - 1) * gw + w, 1), :] * hf + b_f[pl.ds((nr - 1) * gw + w, 1), :]
        hb = a_b[pl.ds(wb, 1), :] * hb + b_b[pl.ds(wb, 1), :]
        return hf, hb

    hf, hb = lax.fori_loop(0, gw, carry_cols, (h0f_ref[0], h0b_ref[0]))
    hfl_ref[0] = hf
    hbl_ref[0] = hb

    def finish(r, carry):
        rows = pl.ds(pl.multiple_of(r * gw, gw), gw)
        hsum = (a_f[rows, :] * hin_f[...] + b_f[rows, :]) + (a_b[rows, :] * hin_b[...] + b_b[rows, :])
        ob_ref[rows, :] = hsum * szb_ref[rows, :]
        return carry

    lax.fori_loop(0, nr, finish, 0)


def _lru_common_specs(n, blk):
    col = lambda *ids: ids[-1]
    return [
        pl.BlockSpec((n, LANES), blk),
        pl.BlockSpec((n, LANES), blk),
        pl.BlockSpec((4, LANES), lambda *ids: (0, col(*ids))),
        pl.BlockSpec((1, LANES), lambda *ids: (0, col(*ids))),
        pl.BlockSpec((1, LANES, 4 * LANES), lambda *ids: (col(*ids), 0, 0)),
        pl.BlockSpec((1, 1, 4 * LANES), lambda *ids: (col(*ids), 0, 0)),
        pl.BlockSpec((1, 1, 2 * LANES), lambda *ids: (col(*ids), 0, 0)),
    ]


def _lru_ctx_call(xb, szb, cw, cb, wcat, bcat, lam, batch):
    n = xb.shape[0]
    seq = n // batch
    nblk = D_MODEL // LANES
    st_spec = pl.BlockSpec((batch, 1, LANES), lambda j: (0, 0, j))
    return pl.pallas_call(
        functools.partial(_lru_ctx_kernel, seq),
        grid=(nblk,),
        in_specs=_lru_common_specs(n, lambda j: (0, j)),
        out_specs=[pl.BlockSpec((n, LANES), lambda j: (0, j)), st_spec, st_spec],
        out_shape=[
            jax.ShapeDtypeStruct((n, D_MODEL), F32),
            jax.ShapeDtypeStruct((batch, 1, D_MODEL), F32),
            jax.ShapeDtypeStruct((batch, 1, D_MODEL), F32),
        ],
        scratch_shapes=[pltpu.VMEM((batch * _ctx_pitch(seq), LANES), F32) for _ in range(4)],
        compiler_params=_params(("arbitrary",)),
        name="lru_ctx",
    )(xb, szb, cw, cb, wcat, bcat, lam)


def _lru_lat_call(xb, szb, cw, cb, wcat, bcat, lam, h0f, h0b, batch):
    n = xb.shape[0] // batch
    nblk = D_MODEL // LANES
    st_spec = pl.BlockSpec((1, 1, LANES), lambda b, j: (b, 0, j))
    return pl.pallas_call(
        _lru_lat_kernel,
        grid=(batch, nblk),
        in_specs=_lru_common_specs(n, lambda b, j: (b, j)) + [st_spec, st_spec],
        out_specs=[pl.BlockSpec((n, LANES), lambda b, j: (b, j)), st_spec, st_spec],
        out_shape=[
            jax.ShapeDtypeStruct((batch * n, D_MODEL), F32),
            jax.ShapeDtypeStruct((batch, 1, D_MODEL), F32),
            jax.ShapeDtypeStruct((batch, 1, D_MODEL), F32),
        ],
        scratch_shapes=[pltpu.VMEM((n + 3 * GRID_W, LANES), F32)]
        + [pltpu.VMEM((n, LANES), F32) for _ in range(4)]
        + [pltpu.VMEM((GRID_W, LANES), F32) for _ in range(2)],
        compiler_params=_params(("arbitrary", "arbitrary")),
        name="lru_lat",
    )(xb, szb, cw, cb, wcat, bcat, lam, h0f, h0b)


def _stage_c_kernel(tiles_per_seq, row0, x_ref, mod_ref, nw_ref, *refs):
    of_refs = refs[:N_HEADS]
    ob_refs = refs[N_HEADS:2 * N_HEADS]
    (sza_ref, olru_ref, wg_ref, bg_ref, wpa_ref, wpb_ref, wo_ref, onw_ref, fnw_ref,
     y_ref) = refs[2 * N_HEADS:]
    row = row0 + pl.program_id(0) // tiles_per_seq
    x = x_ref[...]
    h, gate = _modulated_norm(x, mod_ref, nw_ref, row)
    g = _sigmoid(_bdot(h, wg_ref[...]) + bg_ref[...])
    heads = []
    for hh in range(N_HEADS):
        oh = _chunk_transposed(of_refs[hh]) + _chunk_transposed(ob_refs[hh])
        heads.append(oh * lax.rsqrt(jnp.mean(oh * oh, axis=-1, keepdims=True) + EPS) * onw_ref[...])
    o_a = jnp.concatenate(heads, axis=1) * sza_ref[...]
    pa = _bdot(o_a, wpa_ref[...])
    pb = _bdot(olru_ref[...], wpb_ref[...])
    mixed = _bdot(g[:, 0:D_MODEL] * pa + g[:, D_MODEL:2 * D_MODEL] * pb, wo_ref[...])
    r = x + gate * mixed
    y_ref[...] = r * lax.rsqrt(jnp.mean(r * r, axis=-1, keepdims=True) + EPS) * fnw_ref[...]


def _stage_c_call(x, mod, norm_w, o_f, o_b, sza, olru, wg, bg, wpa, wpb, wo, onw, fnw,
                  tiles_per_seq, row0):
    n = x.shape[0]
    tm = TM_C
    tok = pl.BlockSpec((tm, D_MODEL), lambda i: (i, 0))
    head_cols = [pl.BlockSpec((tm, D_HEAD), lambda i, h=h: (i, h)) for h in range(N_HEADS)]
    return pl.pallas_call(
        functools.partial(_stage_c_kernel, tiles_per_seq, row0),
        grid=(n // tm,),
        in_specs=[
            tok,
            _const_spec((SUBLANES, 3 * D_MODEL)),
            _const_spec((1, D_MODEL)),
        ] + head_cols + head_cols + [
            tok, tok,
            _const_spec((D_MODEL, 2 * D_MODEL)),
            _const_spec((1, 2 * D_MODEL)),
            _const_spec((D_MODEL, D_MODEL)),
            _const_spec((D_MODEL, D_MODEL)),
            _const_spec((D_MODEL, D_MODEL)),
            _const_spec((1, D_HEAD)),
            _const_spec((1, D_MODEL)),
        ],
        out_specs=tok,
        out_shape=jax.ShapeDtypeStruct((n, D_MODEL), F32),
        compiler_params=_params(("arbitrary",)),
        name="stage_c",
    )(x, mod, norm_w, *([o_f] * N_HEADS), *([o_b] * N_HEADS), sza, olru, wg, bg, wpa, wpb, wo, onw, fnw)


def kernel(x_prompt, x_sample, state_a_fwd, state_a_bwd, state_b_fwd, state_b_bwd, c, c_ctx,
           norm_w, w_mod, b_mod, w_in, conv_a_w, conv_a_b, a_log_fwd, dt_bias_fwd, a_log_bwd,
           dt_bias_bwd, onorm_a_w, conv_b_w, conv_b_b, lru_wa_fwd, lru_ba_fwd, lru_wx_fwd,
           lru_bx_fwd, lru_lambda_fwd, lru_wa_bwd, lru_ba_bwd, lru_wx_bwd, lru_bx_bwd,
           lru_lambda_bwd, w_proj_a, w_proj_b, w_gate, b_gate, w_out, final_norm_w):
    bp, seq, d = x_prompt.shape
    bd, dseq, _ = x_sample.shape
    l = 0
    d3 = 3 * D_MODEL

    w_in_l = w_in[l]
    wqkv = w_in_l[:, 0:d3].astype(BF16)
    wza = w_in_l[:, d3:d3 + D_MODEL].astype(BF16)
    wxb = w_in_l[:, d3 + D_MODEL:d3 + 2 * D_MODEL].astype(BF16)
    wzb = w_in_l[:, d3 + 2 * D_MODEL:d3 + 3 * D_MODEL].astype(BF16)
    n_gate_cols = 4 * N_HEADS
    wgc = jnp.pad(w_in_l[:, d3 + 3 * D_MODEL:], ((0, 0), (0, LANES - n_gate_cols))).astype(BF16)
    zeros8 = jnp.zeros((N_HEADS,), F32)
    pad_lanes = jnp.zeros((LANES - n_gate_cols,), F32)
    alog_vec = jnp.concatenate([a_log_fwd[l], zeros8, a_log_bwd[l], zeros8, pad_lanes])[None, :]
    dtb_vec = jnp.concatenate([dt_bias_fwd[l], zeros8, dt_bias_bwd[l], zeros8, pad_lanes])[None, :]
    nw = norm_w[l][None, :]
    cvec = jnp.concatenate([c_ctx[None, :], c, jnp.zeros((SUBLANES - 1 - bd, d), F32)], axis=0)
    conv_aw = conv_a_w[l]
    conv_ab = conv_a_b[l][None, :]
    conv_bw = conv_b_w[l]
    conv_bb = conv_b_b[l][None, :]
    wcat = (0.5 * jnp.concatenate([lru_wa_fwd[l], lru_wx_fwd[l], lru_wa_bwd[l], lru_wx_bwd[l]],
                                  axis=2)).astype(BF16)
    nblk = D_MODEL // LANES
    blk = lambda v: v.reshape(nblk, 1, LANES)
    bcat = 0.5 * jnp.concatenate([blk(lru_ba_fwd[l]), blk(lru_bx_fwd[l]), blk(lru_ba_bwd[l]),
                                  blk(lru_bx_bwd[l])], axis=2)
    lam = jnp.concatenate([blk(lru_lambda_fwd[l]), blk(lru_lambda_bwd[l])], axis=2)
    wg = w_gate[l].astype(BF16)
    bg = b_gate[l][None, :]
    wpa = w_proj_a[l].astype(BF16)
    wpb = w_proj_b[l].astype(BF16)
    wo = w_out[l].astype(BF16)
    onw = onorm_a_w[l][None, :]
    fnw = final_norm_w[None, :]

    mod = _mod_call(cvec, w_mod[l], b_mod[l][None, :])

    def path(x2d, batch, mod_row0, per_seq_mod, tiles_per_seq_c, s_af, s_ab, lru_fn):
        seq_tiles = x2d.shape[0] // batch // TM_A
        qn, kn, vn, sza, xb, szb, gcols = _stage_a_call(
            x2d, mod, nw, wqkv, wza, wxb, wzb, wgc, alog_vec, dtb_vec, conv_aw, conv_ab,
            seq_tiles, per_seq_mod, mod_row0)
        o_f, o_b, saf, sab = _gdn_call(qn, kn, vn, gcols, s_af, s_ab, batch)
        olru, sbf, sbb = lru_fn(xb, szb)
        y = _stage_c_call(x2d, mod, nw, o_f, o_b, sza, olru, wg, bg, wpa, wpb, wo, onw, fnw,
                          tiles_per_seq_c, mod_row0)
        return y, saf, sab, sbf, sbb

    n_ctx = bp * seq
    zero_state = jnp.zeros((bp, N_HEADS, D_HEAD, D_HEAD), F32)
    yp, saf, sab, sbf, sbb = path(
        x_prompt.reshape(n_ctx, d), bp, 0, False, n_ctx // TM_C, zero_state, zero_state,
        lambda xb, szb: _lru_ctx_call(xb, szb, conv_bw, conv_bb, wcat, bcat, lam, bp))

    ys, _, _, _, _ = path(
        x_sample.reshape(bd * dseq, d), bd, 1, True, dseq // TM_C,
        state_a_fwd[:, l], state_a_bwd[:, l],
        lambda xb, szb: _lru_lat_call(xb, szb, conv_bw, conv_bb, wcat, bcat, lam,
                                      state_b_fwd[:, l][:, None, :], state_b_bwd[:, l][:, None, :], bd))

    return (yp.reshape(bp, seq, d), ys.reshape(bd, dseq, d),
            saf[:, None], sab[:, None], sbf, sbb)
```

```python
import functools

import jax
import jax.numpy as jnp
from jax import lax
from jax.experimental import pallas as pl
from jax.experimental.pallas import tpu as pltpu

F32 = jnp.float32
BF16 = jnp.bfloat16
HIGHEST = lax.Precision.HIGHEST

D_MODEL = 1024
N_HEADS = 8
D_HEAD = 128
CHUNK = 64
GRID_W = 64
LRU_C = 8.0
EPS = 1e-6
LANES = 128
SUBLANES = 8
VMEM_LIMIT_BYTES = 56 * 1024 * 1024

TM_A = 256
TM_C = 256
TS_GDN = 256
LRU_GATES_UNROLL = 4


def _bdot(a, b):
    return jnp.dot(a.astype(BF16), b.astype(BF16), preferred_element_type=F32)


def _sigmoid(x):
    return 1.0 / (1.0 + jnp.exp(-x))


def _softplus(x):
    return jnp.maximum(x, 0.0) + jnp.log1p(jnp.exp(-jnp.abs(x)))


def _sigmoid_t(x):
    return 0.5 * jnp.tanh(0.5 * x) + 0.5


def _const_spec(shape):
    n = len(shape)
    return pl.BlockSpec(shape, lambda *_: (0,) * n)


def _params(sem, vmem=VMEM_LIMIT_BYTES):
    return pltpu.CompilerParams(dimension_semantics=sem, vmem_limit_bytes=vmem)


def _mod_kernel(c_ref, w_ref, b_ref, o_ref):
    c = c_ref[...]
    o_ref[...] = _bdot(c * _sigmoid(c), w_ref[...]) + b_ref[...]


def _mod_call(cvec, w_mod, b_mod):
    n_col = 3
    return pl.pallas_call(
        _mod_kernel,
        grid=(n_col,),
        in_specs=[
            _const_spec((SUBLANES, D_MODEL)),
            pl.BlockSpec((D_MODEL, D_MODEL), lambda j: (0, j)),
            pl.BlockSpec((1, D_MODEL), lambda j: (0, j)),
        ],
        out_specs=pl.BlockSpec((SUBLANES, D_MODEL), lambda j: (0, j)),
        out_shape=jax.ShapeDtypeStruct((SUBLANES, 3 * D_MODEL), F32),
        compiler_params=_params(("arbitrary",)),
        name="mod_vectors",
    )(cvec, w_mod, b_mod)


def _chunk_transposed(ref):
    tm = ref.shape[0]
    return jnp.concatenate(
        [ref[pl.ds(c * CHUNK + a, SUBLANES, stride=SUBLANES), :]
         for c in range(tm // CHUNK) for a in range(CHUNK // SUBLANES)], axis=0)


def _chunk_time(pos):
    assert CHUNK == SUBLANES * SUBLANES
    shift = SUBLANES.bit_length() - 1
    return ((pos & (SUBLANES - 1)) << shift) + (pos >> shift)


def _modulated_norm(x, mod_ref, nw_ref, row):
    mod = mod_ref[pl.ds(row, 1), :]
    shift = mod[:, 0:D_MODEL]
    scale = mod[:, D_MODEL:2 * D_MODEL]
    gate = mod[:, 2 * D_MODEL:3 * D_MODEL]
    y = x * lax.rsqrt(jnp.mean(x * x, axis=-1, keepdims=True) + EPS) * nw_ref[...]
    return y * (1.0 + scale) + shift, gate


def _stage_a_kernel(seq_tiles, per_seq_mod, row0, x_ref, *refs):
    ncol = D_MODEL // LANES
    xcol_refs = refs[:ncol]
    (xp_ref, xn_ref, mod_ref, nw_ref, wqkv_ref, wza_ref, wxb_ref, wzb_ref, wgc_ref, alog_ref,
     dtb_ref, cw_ref, cb_ref, q_ref, k_ref, v_ref, sza_ref, xb_ref, szb_ref, gc_ref) = refs[ncol:]
    tm = x_ref.shape[0]
    nck = tm // CHUNK
    i = pl.program_id(0)
    ti = i % seq_tiles
    row = row0 + (i // seq_tiles if per_seq_mod else 0)

    x_t = jnp.concatenate([_chunk_transposed(r) for r in xcol_refs], axis=1)
    x_ext = jnp.concatenate([xp_ref[...], x_t, xn_ref[...]], axis=0)
    h_ext, _ = _modulated_norm(x_ext, mod_ref, nw_ref, row)
    hb_ext = h_ext.astype(BF16)
    sub = lax.broadcasted_iota(jnp.int32, (SUBLANES, D_HEAD), 0)
    down = lambda m, n=1: pltpu.roll(m, n, 0)
    up = lambda m: pltpu.roll(m, SUBLANES - 1, 0)

    def conv_block(proj, col, pcol):
        cs = slice(col, col + D_HEAD)
        w = [cw_ref[t:t + 1, cs] for t in range(4)]
        proj = proj[:, pcol:pcol + D_HEAD]
        xs = [proj[SUBLANES + c * CHUNK:SUBLANES + (c + 1) * CHUNK] for c in range(nck)]
        before = jnp.where(ti == 0, 0.0, proj[0:SUBLANES])
        after = jnp.where(ti == seq_tiles - 1, 0.0,
                          proj[SUBLANES + tm:2 * SUBLANES + tm])
        out = []
        for c in range(nck):
            x = xs[c]
            m1_prev = down(xs[c - 1][CHUNK - 8:CHUNK]) if c > 0 else down(before)
            m2_prev = down(xs[c - 1][CHUNK - 16:CHUNK - 8]) if c > 0 else down(before, 2)
            p1_next = up(xs[c + 1][0:8]) if c < nck - 1 else up(after)
            d7 = jnp.where(sub == 0, m1_prev, down(x[CHUNK - 8:CHUNK]))
            d6 = jnp.where(sub == 0, m2_prev, down(x[CHUNK - 16:CHUNK - 8]))
            u0 = jnp.where(sub == SUBLANES - 1, p1_next, up(x[0:8]))
            xm1 = jnp.concatenate([d7, x[0:CHUNK - 8]], axis=0)
            xm2 = jnp.concatenate([d6, d7, x[0:CHUNK - 16]], axis=0)
            xp1 = jnp.concatenate([x[8:CHUNK], u0], axis=0)
            acc = cb_ref[:, cs] + w[0] * xm2 + w[1] * xm1 + w[2] * x + w[3] * xp1
            out.append(acc * _sigmoid(acc))
        return jnp.concatenate(out, axis=0)

    def l2n(v, scale):
        return v * (lax.rsqrt(jnp.sum(v * v, axis=-1, keepdims=True) + EPS) * scale)

    h_nat, _ = _modulated_norm(x_ref[...], mod_ref, nw_ref, row)
    hb = h_nat.astype(BF16)

    def other_branch(j):
        if j == 0:
            za = jnp.dot(hb, wza_ref[...], preferred_element_type=F32)
            sza_ref[...] = za * _sigmoid(za)
        elif j == 1:
            xb_ref[...] = jnp.dot(hb, wxb_ref[...], preferred_element_type=F32)
        else:
            zb = jnp.dot(hb, wzb_ref[...], preferred_element_type=F32)
            szb_ref[...] = zb * _sigmoid(zb)

    post = [lambda t: l2n(t, D_HEAD ** -0.5), lambda t: l2n(t, 1.0), lambda t: t]
    for j, out_ref in enumerate((q_ref, k_ref, v_ref)):
        proj = jnp.dot(hb_ext, wqkv_ref[:, j * D_MODEL:(j + 1) * D_MODEL],
                       preferred_element_type=F32)
        for h in range(N_HEADS):
            hs = slice(h * D_HEAD, (h + 1) * D_HEAD)
            out_ref[:, hs] = post[j](conv_block(proj, j * D_MODEL + h * D_HEAD,
                                                h * D_HEAD)).astype(BF16)
        other_branch(j)

    gc = jnp.dot(hb_ext[SUBLANES:SUBLANES + tm], wgc_ref[...], preferred_element_type=F32)
    lane = lax.broadcasted_iota(jnp.int32, gc.shape, 1)
    is_decay = (lane < 8) | ((lane >= 16) & (lane < 24))
    log_decay = -jnp.exp(alog_ref[...]) * _softplus(gc + dtb_ref[...])
    gc_ref[...] = jnp.where(is_decay, log_decay, _sigmoid(gc))


def _stage_a_call(x, mod, norm_w, wqkv, wza, wxb, wzb, wgc, alog_vec, dtb_vec, conv_w, conv_b,
                  seq_tiles, per_seq_mod, row0):
    n = x.shape[0]
    tm = TM_A
    blocks8 = tm // SUBLANES
    ncol = D_MODEL // LANES
    tok = lambda w: pl.BlockSpec((tm, w), lambda i: (i, 0))
    return pl.pallas_call(
        functools.partial(_stage_a_kernel, seq_tiles, per_seq_mod, row0),
        grid=(n // tm,),
        in_specs=[tok(D_MODEL)]
        + [pl.BlockSpec((tm, LANES), lambda i, j=j: (i, j)) for j in range(ncol)]
        + [
            pl.BlockSpec((SUBLANES, D_MODEL), lambda i: (jnp.maximum(i * blocks8 - 1, 0), 0)),
            pl.BlockSpec((SUBLANES, D_MODEL),
                         lambda i: (jnp.minimum((i + 1) * blocks8, n // SUBLANES - 1), 0)),
            _const_spec((SUBLANES, 3 * D_MODEL)),
            _const_spec((1, D_MODEL)),
            _const_spec((D_MODEL, 3 * D_MODEL)),
            _const_spec((D_MODEL, D_MODEL)),
            _const_spec((D_MODEL, D_MODEL)),
            _const_spec((D_MODEL, D_MODEL)),
            _const_spec((D_MODEL, LANES)),
            _const_spec((1, LANES)),
            _const_spec((1, LANES)),
            _const_spec((4, 3 * D_MODEL)),
            _const_spec((1, 3 * D_MODEL)),
        ],
        out_specs=[tok(D_MODEL), tok(D_MODEL), tok(D_MODEL), tok(D_MODEL), tok(D_MODEL),
                   tok(D_MODEL), tok(LANES)],
        out_shape=[
            jax.ShapeDtypeStruct((n, D_MODEL), BF16),
            jax.ShapeDtypeStruct((n, D_MODEL), BF16),
            jax.ShapeDtypeStruct((n, D_MODEL), BF16),
            jax.ShapeDtypeStruct((n, D_MODEL), F32),
            jax.ShapeDtypeStruct((n, D_MODEL), F32),
            jax.ShapeDtypeStruct((n, D_MODEL), F32),
            jax.ShapeDtypeStruct((n, LANES), F32),
        ],
        compiler_params=_params(("arbitrary",)),
        name="stage_a",
    )(x, *([x] * ncol), x, x, mod, norm_w, wqkv, wza, wxb, wzb, wgc, alog_vec, dtb_vec, conv_w, conv_b)


def _chunk_cumsum(x, reverse):
    nv = CHUNK // SUBLANES
    v = [x[a * SUBLANES:(a + 1) * SUBLANES] for a in range(nv)]
    order = range(nv - 2, -1, -1) if reverse else range(1, nv)
    for a in order:
        v[a] = v[a] + v[a + 1 if reverse else a - 1]
    tot = v[0] if reverse else v[nv - 1]
    sub = lax.broadcasted_iota(jnp.int32, tot.shape, 0)
    acc = tot
    s = 1
    while s < SUBLANES:
        if reverse:
            acc = acc + jnp.where(sub < SUBLANES - s, pltpu.roll(acc, SUBLANES - s, 0), 0.0)
        else:
            acc = acc + jnp.where(sub >= s, pltpu.roll(acc, s, 0), 0.0)
        s *= 2
    if reverse:
        carry = jnp.where(sub < SUBLANES - 1, pltpu.roll(acc, SUBLANES - 1, 0), 0.0)
    else:
        carry = jnp.where(sub >= 1, pltpu.roll(acc, 1, 0), 0.0)
    return jnp.concatenate([p + carry for p in v], axis=0)


def _run_interleaved(gens):
    gens = list(gens)
    while gens:
        alive = []
        for g in gens:
            try:
                next(g)
                alive.append(g)
            except StopIteration:
                pass
        gens = alive


def _delta_rule_tile(reverse, q_ref, k_ref, v_ref, gc_ref, state, o_ref,
                     u_scr, wq_scr, qk_scr, kd_scr, gt_scr):
    ts = q_ref.shape[0]
    nc = ts // CHUNK
    nh = N_HEADS
    npair = nh // 2
    pshape = (CHUNK, 2 * CHUNK)
    lane_i = lax.broadcasted_iota(jnp.int32, pshape, 1)
    left = lane_i < CHUNK
    ri = _chunk_time(lax.broadcasted_iota(jnp.int32, pshape, 0))
    ci = _chunk_time(lane_i & (CHUNK - 1))
    if reverse:
        ri, ci = ci, ri
    incl = ri >= ci
    strict = ri > ci
    eye = ri == ci
    level_masks = []
    for lg in range(6):
        s = 1 << lg
        same = (ri >> (lg + 1)) == (ci >> (lg + 1))
        level_masks.append(same & ((ri & (2 * s - 1)) >= s) & ((ci & (2 * s - 1)) < s))
    g_off = 16 if reverse else 0
    b_off = 24 if reverse else 8

    pairs = [(c, p) for c in range(nc) for p in range(npair)]
    rows = lambda c: slice(c * CHUNK, (c + 1) * CHUNK)
    pcols = lambda p: slice(2 * p * D_HEAD, (2 * p + 2) * D_HEAD)
    lane = lambda off, h: slice(off + h, off + h + 1)
    halves = lambda m: (m[:, 0:D_HEAD], m[:, D_HEAD:2 * D_HEAD])
    zero_bf = jnp.zeros((CHUNK, D_HEAD), BF16)

    def blockdiag(m):
        return jnp.concatenate([jnp.where(left, m, 0.0), jnp.where(left, 0.0, m)],
                               axis=0).astype(BF16)

    def blockdiag_wide(m0, m1):
        return jnp.concatenate([jnp.concatenate([m0, zero_bf], axis=1),
                                jnp.concatenate([zero_bf, m1], axis=1)], axis=0)

    gcol = [gc_ref[rows(c), :] for c in range(nc)]
    g_cum = [_chunk_cumsum(g, reverse) for g in gcol]
    g_cum_t = [jnp.concatenate([g, g], axis=0).T for g in g_cum]
    last = 0 if reverse else CHUNK - 1
    g_tot = [g[last:last + 1, :] for g in g_cum]
    for c in range(nc):
        gt_scr[c] = jnp.broadcast_to(g_tot[c], (SUBLANES, LANES))
    yield
    g_i1 = [[g_cum[c][:, lane(g_off, 2 * p + e)] for e in range(2)] for c, p in pairs]
    beta = [[gcol[c][:, lane(b_off, 2 * p + e)] for e in range(2)] for c, p in pairs]
    g_i = [jnp.where(left, a, b) for a, b in g_i1]
    g_j = [jnp.where(left[0:1], g_cum_t[c][lane(g_off, 2 * p), :],
                     g_cum_t[c][lane(g_off, 2 * p + 1), :]) for c, p in pairs]
    k = [halves(k_ref[rows(c), pcols(p)].astype(F32)) for c, p in pairs]
    q = [halves(q_ref[rows(c), pcols(p)].astype(F32)) for c, p in pairs]
    kb = [(k_[0] * b[0], k_[1] * b[1]) for k_, b in zip(k, beta)]
    prod = [lax.dot_general(
        jnp.concatenate([jnp.concatenate(kb_, axis=1), jnp.concatenate(q_, axis=1)],
                        axis=0).astype(BF16),
        blockdiag_wide(k_[0].astype(BF16), k_[1].astype(BF16)),
        (((1,), (1,)), ((), ())), preferred_element_type=F32)
        for kb_, q_, k_ in zip(kb, q, k)]
    yield
    decay = [jnp.exp(jnp.where(incl, a - b, -1e30)) for a, b in zip(g_i, g_j)]
    a_mat = [jnp.where(strict, p_[0:CHUNK] * d, 0.0) for p_, d in zip(prod, decay)]
    for i, (p_, d) in enumerate(zip(prod, decay)):
        qk_scr[i] = (p_[CHUNK:2 * CHUNK] * d).astype(BF16)
    t_inv = [jnp.where(eye, 1.0, 0.0) - jnp.where(level_masks[0], a, 0.0) for a in a_mat]
    yield
    for lvl in range(1, 6):
        x = [jnp.dot(jnp.where(level_masks[lvl], a, 0.0).astype(BF16), blockdiag(t),
                     preferred_element_type=F32) for a, t in zip(a_mat, t_inv)]
        yield
        t_inv = [t - jnp.dot(t.astype(BF16), blockdiag(x_), preferred_element_type=F32)
                 for t, x_ in zip(t_inv, x)]
        yield
    t_off = [jnp.where(eye, 0.0, t).astype(BF16) for t in t_inv]
    e_g = [[jnp.exp(g) for g in gp] for gp in g_i1]
    v = [halves(v_ref[rows(c), pcols(p)].astype(F32)) for c, p in pairs]
    for i, (c, p) in enumerate(pairs):
        for e in range(2):
            rhs = jnp.concatenate([v[i][e] * beta[i][e], kb[i][e] * e_g[i][e]], axis=1)
            rhs_bf = rhs.astype(BF16)
            zeros = jnp.zeros_like(rhs_bf)
            stacked = jnp.concatenate([rhs_bf, zeros] if e == 0 else [zeros, rhs_bf], axis=0)
            sol = rhs + jnp.dot(t_off[i], stacked, preferred_element_type=F32)
            j = 2 * i + e
            u_scr[j] = sol[:, 0:D_HEAD]
            wq_scr[j] = jnp.concatenate([sol[:, D_HEAD:2 * D_HEAD], q[i][e] * e_g[i][e]],
                                        axis=0).astype(BF16)
            kd_scr[j] = (k[i][e] * jnp.exp(g_tot[c][:, lane(g_off, 2 * p + e)]
                                           - g_i1[i][e])).astype(BF16)
    yield

    for c in (range(nc - 1, -1, -1) if reverse else range(nc)):
        idx = [c * nh + h for h in range(nh)]
        s_old = [state[h] for h in range(nh)]
        ws_qs = [jnp.dot(wq_scr[i], s.astype(BF16), preferred_element_type=F32)
                 for i, s in zip(idx, s_old)]
        yield
        v_new = [(u_scr[i] - r[0:CHUNK]).astype(BF16) for i, r in zip(idx, ws_qs)]
        for p in range(npair):
            h0, h1 = 2 * p, 2 * p + 1
            o_ref[rows(c), pcols(p)] = jnp.concatenate(
                [ws_qs[h0][CHUNK:2 * CHUNK], ws_qs[h1][CHUNK:2 * CHUNK]], axis=1) + jnp.dot(
                qk_scr[c * npair + p], blockdiag_wide(v_new[h0], v_new[h1]),
                preferred_element_type=F32)
        yield
        for h in range(nh):
            e_tot = jnp.exp(gt_scr[c][0:1, lane(g_off, h)])
            state[h] = s_old[h] * e_tot + lax.dot_general(
                kd_scr[idx[h]], v_new[h], (((0,), (0,)), ((), ())), preferred_element_type=F32)
        yield


def _gdn_scratch(ts):
    nprob = (ts // CHUNK) * N_HEADS
    return [
        pltpu.VMEM((N_HEADS, D_HEAD, D_HEAD), F32),
        pltpu.VMEM((nprob, CHUNK, D_HEAD), F32),
        pltpu.VMEM((nprob, 2 * CHUNK, D_HEAD), BF16),
        pltpu.VMEM((nprob // 2, CHUNK, 2 * CHUNK), BF16),
        pltpu.VMEM((nprob, CHUNK, D_HEAD), BF16),
        pltpu.VMEM((ts // CHUNK, SUBLANES, LANES), F32),
    ]


def _gdn_kernel(nt, qf_ref, kf_ref, vf_ref, gcf_ref, qb_ref, kb_ref, vb_ref, gcb_ref, s0f_ref, s0b_ref,
                of_ref, ob_ref, sff_ref, sfb_ref, *scr):
    half = len(scr) // 2
    scr_f, scr_b = scr[:half], scr[half:]
    t = pl.program_id(1)

    @pl.when(t == 0)
    def _init():
        scr_f[0][...] = s0f_ref[0]
        scr_b[0][...] = s0b_ref[0]

    _run_interleaved([
        _delta_rule_tile(False, qf_ref, kf_ref, vf_ref, gcf_ref, scr_f[0], of_ref, *scr_f[1:]),
        _delta_rule_tile(True, qb_ref, kb_ref, vb_ref, gcb_ref, scr_b[0], ob_ref, *scr_b[1:]),
    ])

    @pl.when(t == nt - 1)
    def _fin():
        sff_ref[0] = scr_f[0][...]
        sfb_ref[0] = scr_b[0][...]


def _gdn_call(q, k, v, gcols, s0f, s0b, batch):
    n = q.shape[0]
    ts = TS_GDN
    nt = n // batch // ts
    state_spec = pl.BlockSpec((1, N_HEADS, D_HEAD, D_HEAD), lambda b, t: (b, 0, 0, 0))
    tok_f = pl.BlockSpec((ts, D_MODEL), lambda b, t: (b * nt + t, 0))
    tok_b = pl.BlockSpec((ts, D_MODEL), lambda b, t: (b * nt + nt - 1 - t, 0))
    gc_f = pl.BlockSpec((ts, LANES), lambda b, t: (b * nt + t, 0))
    gc_b = pl.BlockSpec((ts, LANES), lambda b, t: (b * nt + nt - 1 - t, 0))
    state_shape = jax.ShapeDtypeStruct((batch, N_HEADS, D_HEAD, D_HEAD), F32)
    return pl.pallas_call(
        functools.partial(_gdn_kernel, nt),
        grid=(batch, nt),
        in_specs=[tok_f, tok_f, tok_f, gc_f, tok_b, tok_b, tok_b, gc_b, state_spec, state_spec],
        out_specs=[tok_f, tok_b, state_spec, state_spec],
        out_shape=[
            jax.ShapeDtypeStruct((n, D_MODEL), F32),
            jax.ShapeDtypeStruct((n, D_MODEL), F32),
            state_shape,
            state_shape,
        ],
        scratch_shapes=_gdn_scratch(ts) + _gdn_scratch(ts),
        compiler_params=_params(("arbitrary", "arbitrary")),
        name="gdn",
    )(q, k, v, gcols, q, k, v, gcols, s0f, s0b)


def _lru_gates(xc, wcat_ref, bcat_ref, lam_ref):
    th = jnp.tanh(_bdot(xc, wcat_ref[0]) + bcat_ref[0])
    half_c_sp = (-0.5 * LRU_C) * _softplus(-lam_ref[0])
    half_x = 0.5 * xc
    out = []
    for d in range(2):
        th_r = th[:, (2 * d) * LANES:(2 * d + 1) * LANES]
        th_i = th[:, (2 * d + 1) * LANES:(2 * d + 2) * LANES]
        hcs = half_c_sp[:, d * LANES:(d + 1) * LANES]
        log_a = th_r * hcs + hcs
        a = jnp.exp(log_a)
        one_m_a2 = jnp.tanh(log_a) * (-1.0 - a * a)
        root = jnp.where(one_m_a2 > 0.0, one_m_a2 * lax.rsqrt(one_m_a2), 0.0)
        out.append((a, root * ((th_i + 1.0) * half_x)))
    return out


def _ctx_pitch(seq):
    tiles = seq // SUBLANES
    return (tiles + 1 - tiles % 2) * SUBLANES


def _lru_ctx_kernel(seq, xb_ref, szb_ref, cw_ref, cb_ref, wcat_ref, bcat_ref, lam_ref,
                    ob_ref, hfl_ref, hbl_ref, a_f, b_f, a_b, b_b):
    nb = xb_ref.shape[0] // seq
    pitch = _ctx_pitch(seq)
    rowi = lax.broadcasted_iota(jnp.int32, (seq, 1), 0)

    def per_seq(b, carry):
        x = xb_ref[pl.ds(pl.multiple_of(b * seq, seq), seq), :]
        r0 = pl.multiple_of(b * pitch, SUBLANES)
        xm2 = jnp.where(rowi >= 2, pltpu.roll(x, 2, 0), 0.0)
        xm1 = jnp.where(rowi >= 1, pltpu.roll(x, 1, 0), 0.0)
        xp1 = jnp.where(rowi <= seq - 2, pltpu.roll(x, seq - 1, 0), 0.0)
        xc = (cw_ref[0:1, :] * xm2 + cw_ref[1:2, :] * xm1 + cw_ref[2:3, :] * x
              + cw_ref[3:4, :] * xp1 + cb_ref[...])
        (af, bf), (ab, bb) = _lru_gates(xc, wcat_ref, bcat_ref, lam_ref)
        a_f[pl.ds(r0, seq), :] = af
        b_f[pl.ds(r0, seq), :] = bf
        a_b[pl.ds(r0, seq), :] = ab
        b_b[pl.ds(r0, seq), :] = bb
        return carry

    lax.fori_loop(0, nb, per_seq, 0, unroll=LRU_GATES_UNROLL)

    def step(t, carry):
        hf, hb = carry
        rows_f = pl.ds(t, nb, stride=pitch)
        rows_b = pl.ds(seq - 1 - t, nb, stride=pitch)
        hf = a_f[rows_f, :] * hf + b_f[rows_f, :]
        hb = a_b[rows_b, :] * hb + b_b[rows_b, :]
        a_f[rows_f, :] = hf
        a_b[rows_b, :] = hb
        return hf, hb

    zero = jnp.zeros((nb, LANES), F32)
    lax.fori_loop(0, seq, step, (zero, zero))

    def write(b, carry):
        rows = pl.ds(pl.multiple_of(b * seq, seq), seq)
        prow = pl.ds(pl.multiple_of(b * pitch, SUBLANES), seq)
        ob_ref[rows, :] = (a_f[prow, :] + a_b[prow, :]) * szb_ref[rows, :]
        return carry

    lax.fori_loop(0, nb, write, 0)
    for b in range(nb):
        hfl_ref[b] = a_f[b * pitch + seq - 1:b * pitch + seq, :]
        hbl_ref[b] = a_b[b * pitch:b * pitch + 1, :]


def _lru_lat_kernel(xb_ref, szb_ref, cw_ref, cb_ref, wcat_ref, bcat_ref, lam_ref, h0f_ref, h0b_ref,
                    ob_ref, hfl_ref, hbl_ref, xext, a_f, b_f, a_b, b_b, hin_f, hin_b):
    gw = GRID_W
    n = xb_ref.shape[0]
    nr = n // gw
    wi = lax.broadcasted_iota(jnp.int32, (gw, 1), 0)
    xext[2 * gw:2 * gw + n, :] = xb_ref[...]
    xext[0:gw, :] = jnp.where(wi >= 1, pltpu.roll(xb_ref[(nr - 2) * gw:(nr - 1) * gw, :], 1, 0), 0.0)
    xext[gw:2 * gw, :] = jnp.where(wi >= 1, pltpu.roll(xb_ref[(nr - 1) * gw:nr * gw, :], 1, 0), 0.0)
    xext[2 * gw + n:3 * gw + n, :] = jnp.where(wi <= gw - 2, pltpu.roll(xb_ref[0:gw, :], gw - 1, 0), 0.0)

    rows_per = 256

    def gates(q, carry):
        r0 = pl.multiple_of(q * rows_per, rows_per)
        xc = cb_ref[...] + cw_ref[0:1, :] * xext[pl.ds(r0, rows_per), :]
        for i in range(1, 4):
            xc = xc + cw_ref[i:i + 1, :] * xext[pl.ds(pl.multiple_of(r0 + i * gw, gw), rows_per), :]
        (af, bf), (ab, bb) = _lru_gates(xc, wcat_ref, bcat_ref, lam_ref)
        a_f[pl.ds(r0, rows_per), :] = af
        b_f[pl.ds(r0, rows_per), :] = bf
        a_b[pl.ds(r0, rows_per), :] = ab
        b_b[pl.ds(r0, rows_per), :] = bb
        return carry

    lax.fori_loop(0, n // rows_per, gates, 0, unroll=LRU_GATES_UNROLL)

    ng = gw // SUBLANES

    def col_scan(r, carry):
        acf, bcf, acb, bcb = carry
        rb = nr - 1 - r
        nacf, nbcf, nacb, nbcb = [], [], [], []
        for g in range(ng):
            rows_f = pl.ds(pl.multiple_of(r * gw + g * SUBLANES, SUBLANES), SUBLANES)
            rows_b = pl.ds(pl.multiple_of(rb * gw + g * SUBLANES, SUBLANES), SUBLANES)
            af = a_f[rows_f, :]
            ab = a_b[rows_b, :]
            caf = af * acf[g]
            cbf = af * bcf[g] + b_f[rows_f, :]
            cab = ab * acb[g]
            cbb = ab * bcb[g] + b_b[rows_b, :]
            a_f[rows_f, :] = caf
            b_f[rows_f, :] = cbf
            a_b[rows_b, :] = cab
            b_b[rows_b, :] = cbb
            nacf.append(caf)
            nbcf.append(cbf)
            nacb.append(cab)
            nbcb.append(cbb)
        return tuple(nacf), tuple(nbcf), tuple(nacb), tuple(nbcb)

    one = tuple(jnp.ones((SUBLANES, LANES), F32) for _ in range(ng))
    zero = tuple(jnp.zeros((SUBLANES, LANES), F32) for _ in range(ng))
    lax.fori_loop(0, nr, col_scan, (one, zero, one, zero))

    def carry_cols(w, carry):
        hf, hb = carry
        wb = gw - 1 - w
        hin_f[pl.ds(w, 1), :] = hf
        hin_b[pl.ds(wb, 1), :] = hb
        hf = a_f[pl.ds((nr - 1) * gw + w, 1), :] * hf + b_f[pl.ds((nr - 1) * gw + w, 1), :]
        hb = a_b[pl.ds(wb, 1), :] * hb + b_b[pl.ds(wb, 1), :]
        return hf, hb

    hf, hb = lax.fori_loop(0, gw, carry_cols, (h0f_ref[0], h0b_ref[0]))
    hfl_ref[0] = hf
    hbl_ref[0] = hb

    def finish(r, carry):
        rows = pl.ds(pl.multiple_of(r * gw, gw), gw)
        hsum = (a_f[rows, :] * hin_f[...] + b_f[rows, :]) + (a_b[rows, :] * hin_b[...] + b_b[rows, :])
        ob_ref[rows, :] = hsum * szb_ref[rows, :]
        return carry

    lax.fori_loop(0, nr, finish, 0)


def _lru_common_specs(n, blk):
    col = lambda *ids: ids[-1]
    return [
        pl.BlockSpec((n, LANES), blk),
        pl.BlockSpec((n, LANES), blk),
        pl.BlockSpec((4, LANES), lambda *ids: (0, col(*ids))),
        pl.BlockSpec((1, LANES), lambda *ids: (0, col(*ids))),
        pl.BlockSpec((1, LANES, 4 * LANES), lambda *ids: (col(*ids), 0, 0)),
        pl.BlockSpec((1, 1, 4 * LANES), lambda *ids: (col(*ids), 0, 0)),
        pl.BlockSpec((1, 1, 2 * LANES), lambda *ids: (col(*ids), 0, 0)),
    ]


def _lru_ctx_call(xb, szb, cw, cb, wcat, bcat, lam, batch):
    n = xb.shape[0]
    seq = n // batch
    nblk = D_MODEL // LANES
    st_spec = pl.BlockSpec((batch, 1, LANES), lambda j: (0, 0, j))
    return pl.pallas_call(
        functools.partial(_lru_ctx_kernel, seq),
        grid=(nblk,),
        in_specs=_lru_common_specs(n, lambda j: (0, j)),
        out_specs=[pl.BlockSpec((n, LANES), lambda j: (0, j)), st_spec, st_spec],
        out_shape=[
            jax.ShapeDtypeStruct((n, D_MODEL), F32),
            jax.ShapeDtypeStruct((batch, 1, D_MODEL), F32),
            jax.ShapeDtypeStruct((batch, 1, D_MODEL), F32),
        ],
        scratch_shapes=[pltpu.VMEM((batch * _ctx_pitch(seq), LANES), F32) for _ in range(4)],
        compiler_params=_params(("arbitrary",)),
        name="lru_ctx",
    )(xb, szb, cw, cb, wcat, bcat, lam)


def _lru_lat_call(xb, szb, cw, cb, wcat, bcat, lam, h0f, h0b, batch):
    n = xb.shape[0] // batch
    nblk = D_MODEL // LANES
    st_spec = pl.BlockSpec((1, 1, LANES), lambda b, j: (b, 0, j))
    return pl.pallas_call(
        _lru_lat_kernel,
        grid=(batch, nblk),
        in_specs=_lru_common_specs(n, lambda b, j: (b, j)) + [st_spec, st_spec],
        out_specs=[pl.BlockSpec((n, LANES), lambda b, j: (b, j)), st_spec, st_spec],
        out_shape=[
            jax.ShapeDtypeStruct((batch * n, D_MODEL), F32),
            jax.ShapeDtypeStruct((batch, 1, D_MODEL), F32),
            jax.ShapeDtypeStruct((batch, 1, D_MODEL), F32),
        ],
        scratch_shapes=[pltpu.VMEM((n + 3 * GRID_W, LANES), F32)]
        + [pltpu.VMEM((n, LANES), F32) for _ in range(4)]
        + [pltpu.VMEM((GRID_W, LANES), F32) for _ in range(2)],
        compiler_params=_params(("arbitrary", "arbitrary")),
        name="lru_lat",
    )(xb, szb, cw, cb, wcat, bcat, lam, h0f, h0b)


def _stage_c_kernel(tiles_per_seq, row0, x_ref, mod_ref, nw_ref, *refs):
    of_refs = refs[:N_HEADS]
    ob_refs = refs[N_HEADS:2 * N_HEADS]
    (sza_ref, olru_ref, wg_ref, bg_ref, wpa_ref, wpb_ref, wo_ref, onw_ref, fnw_ref,
     y_ref) = refs[2 * N_HEADS:]
    row = row0 + pl.program_id(0) // tiles_per_seq
    x = x_ref[...]
    h, gate = _modulated_norm(x, mod_ref, nw_ref, row)
    g = _sigmoid(_bdot(h, wg_ref[...]) + bg_ref[...])
    heads = []
    for hh in range(N_HEADS):
        oh = _chunk_transposed(of_refs[hh]) + _chunk_transposed(ob_refs[hh])
        heads.append(oh * lax.rsqrt(jnp.mean(oh * oh, axis=-1, keepdims=True) + EPS) * onw_ref[...])
    o_a = jnp.concatenate(heads, axis=1) * sza_ref[...]
    pa = _bdot(o_a, wpa_ref[...])
    pb = _bdot(olru_ref[...], wpb_ref[...])
    mixed = _bdot(g[:, 0:D_MODEL] * pa + g[:, D_MODEL:2 * D_MODEL] * pb, wo_ref[...])
    r = x + gate * mixed
    y_ref[...] = r * lax.rsqrt(jnp.mean(r * r, axis=-1, keepdims=True) + EPS) * fnw_ref[...]


def _stage_c_call(x, mod, norm_w, o_f, o_b, sza, olru, wg, bg, wpa, wpb, wo, onw, fnw,
                  tiles_per_seq, row0):
    n = x.shape[0]
    tm = TM_C
    tok = pl.BlockSpec((tm, D_MODEL), lambda i: (i, 0))
    head_cols = [pl.BlockSpec((tm, D_HEAD), lambda i, h=h: (i, h)) for h in range(N_HEADS)]
    return pl.pallas_call(
        functools.partial(_stage_c_kernel, tiles_per_seq, row0),
        grid=(n // tm,),
        in_specs=[
            tok,
            _const_spec((SUBLANES, 3 * D_MODEL)),
            _const_spec((1, D_MODEL)),
        ] + head_cols + head_cols + [
            tok, tok,
            _const_spec((D_MODEL, 2 * D_MODEL)),
            _const_spec((1, 2 * D_MODEL)),
            _const_spec((D_MODEL, D_MODEL)),
            _const_spec((D_MODEL, D_MODEL)),
            _const_spec((D_MODEL, D_MODEL)),
            _const_spec((1, D_HEAD)),
            _const_spec((1, D_MODEL)),
        ],
        out_specs=tok,
        out_shape=jax.ShapeDtypeStruct((n, D_MODEL), F32),
        compiler_params=_params(("arbitrary",)),
        name="stage_c",
    )(x, mod, norm_w, *([o_f] * N_HEADS), *([o_b] * N_HEADS), sza, olru, wg, bg, wpa, wpb, wo, onw, fnw)


def kernel(x_prompt, x_sample, state_a_fwd, state_a_bwd, state_b_fwd, state_b_bwd, c, c_ctx,
           norm_w, w_mod, b_mod, w_in, conv_a_w, conv_a_b, a_log_fwd, dt_bias_fwd, a_log_bwd,
           dt_bias_bwd, onorm_a_w, conv_b_w, conv_b_b, lru_wa_fwd, lru_ba_fwd, lru_wx_fwd,
           lru_bx_fwd, lru_lambda_fwd, lru_wa_bwd, lru_ba_bwd, lru_wx_bwd, lru_bx_bwd,
           lru_lambda_bwd, w_proj_a, w_proj_b, w_gate, b_gate, w_out, final_norm_w):
    bp, seq, d = x_prompt.shape
    bd, dseq, _ = x_sample.shape
    l = 0
    d3 = 3 * D_MODEL

    w_in_l = w_in[l]
    wqkv = w_in_l[:, 0:d3].astype(BF16)
    wza = w_in_l[:, d3:d3 + D_MODEL].astype(BF16)
    wxb = w_in_l[:, d3 + D_MODEL:d3 + 2 * D_MODEL].astype(BF16)
    wzb = w_in_l[:, d3 + 2 * D_MODEL:d3 + 3 * D_MODEL].astype(BF16)
    n_gate_cols = 4 * N_HEADS
    wgc = jnp.pad(w_in_l[:, d3 + 3 * D_MODEL:], ((0, 0), (0, LANES - n_gate_cols))).astype(BF16)
    zeros8 = jnp.zeros((N_HEADS,), F32)
    pad_lanes = jnp.zeros((LANES - n_gate_cols,), F32)
    alog_vec = jnp.concatenate([a_log_fwd[l], zeros8, a_log_bwd[l], zeros8, pad_lanes])[None, :]
    dtb_vec = jnp.concatenate([dt_bias_fwd[l], zeros8, dt_bias_bwd[l], zeros8, pad_lanes])[None, :]
    nw = norm_w[l][None, :]
    cvec = jnp.concatenate([c_ctx[None, :], c, jnp.zeros((SUBLANES - 1 - bd, d), F32)], axis=0)
    conv_aw = conv_a_w[l]
    conv_ab = conv_a_b[l][None, :]
    conv_bw = conv_b_w[l]
    conv_bb = conv_b_b[l][None, :]
    wcat = (0.5 * jnp.concatenate([lru_wa_fwd[l], lru_wx_fwd[l], lru_wa_bwd[l], lru_wx_bwd[l]],
                                  axis=2)).astype(BF16)
    nblk = D_MODEL // LANES
    blk = lambda v: v.reshape(nblk, 1, LANES)
    bcat = 0.5 * jnp.concatenate([blk(lru_ba_fwd[l]), blk(lru_bx_fwd[l]), blk(lru_ba_bwd[l]),
                                  blk(lru_bx_bwd[l])], axis=2)
    lam = jnp.concatenate([blk(lru_lambda_fwd[l]), blk(lru_lambda_bwd[l])], axis=2)
    wg = w_gate[l].astype(BF16)
    bg = b_gate[l][None, :]
    wpa = w_proj_a[l].astype(BF16)
    wpb = w_proj_b[l].astype(BF16)
    wo = w_out[l].astype(BF16)
    onw = onorm_a_w[l][None, :]
    fnw = final_norm_w[None, :]

    mod = _mod_call(cvec, w_mod[l], b_mod[l][None, :])

    def path(x2d, batch, mod_row0, per_seq_mod, tiles_per_seq_c, s_af, s_ab, lru_fn):
        seq_tiles = x2d.shape[0] // batch // TM_A
        qn, kn, vn, sza, xb, szb, gcols = _stage_a_call(
            x2d, mod, nw, wqkv, wza, wxb, wzb, wgc, alog_vec, dtb_vec, conv_aw, conv_ab,
            seq_tiles, per_seq_mod, mod_row0)
        o_f, o_b, saf, sab = _gdn_call(qn, kn, vn, gcols, s_af, s_ab, batch)
        olru, sbf, sbb = lru_fn(xb, szb)
        y = _stage_c_call(x2d, mod, nw, o_f, o_b, sza, olru, wg, bg, wpa, wpb, wo, onw, fnw,
                          tiles_per_seq_c, mod_row0)
        return y, saf, sab, sbf, sbb

    n_ctx = bp * seq
    zero_state = jnp.zeros((bp, N_HEADS, D_HEAD, D_HEAD), F32)
    yp, saf, sab, sbf, sbb = path(
        x_prompt.reshape(n_ctx, d), bp, 0, False, n_ctx // TM_C, zero_state, zero_state,
        lambda xb, szb: _lru_ctx_call(xb, szb, conv_bw, conv_bb, wcat, bcat, lam, bp))

    ys, _, _, _, _ = path(
        x_sample.reshape(bd * dseq, d), bd, 1, True, dseq // TM_C,
        state_a_fwd[:, l], state_a_bwd[:, l],
        lambda xb, szb: _lru_lat_call(xb, szb, conv_bw, conv_bb, wcat, bcat, lam,
                                      state_b_fwd[:, l][:, None, :], state_b_bwd[:, l][:, None, :], bd))

    return (yp.reshape(bp, seq, d), ys.reshape(bd, dseq, d),
            saf[:, None], sab[:, None], sbf, sbb)
```
